```python
import jax, jax.numpy as jnp
from jax import lax
import numpy as np

D_MODEL = 2048
BATCH = 4
SEQ = 2048
DEPTH = 4
DEC_BATCH = 128
DEC_SEQ = 4
PAST_LEN = 16384
PAGE_SIZE = 128

N_MIXERS = 3
N_A = (DEPTH + 2) // 3
N_B = (DEPTH + 1) // 3
N_C = DEPTH // 3
N_DENSE = (DEPTH + 1) // 2
N_MOE = DEPTH // 2

CHUNK = 128
A_GROUPS = 8
A_WIDTH = D_MODEL
A_GROUP_DIM = A_WIDTH // A_GROUPS
CONV_W = 3
CONV_DIM = D_MODEL
POOL_WINDOWS = (2, 4, 8, 16)
POOL_GROUPS = len(POOL_WINDOWS)
POOL_GROUP_DIM = D_MODEL // POOL_GROUPS
POOL_BUF = max(POOL_WINDOWS) - 1
D_FF = 7 * D_MODEL // 2
N_EXPERTS = 8
TOP_K = 2
EPS = 1e-6

kernel_name = 'hybrid_chunkmlp_shortconv_pool_moe_step'

F32 = jnp.float32


def rmsnorm(x, g):
    xf = x.astype(F32)
    y = xf * lax.rsqrt(jnp.mean(xf * xf, axis=-1, keepdims=True) + EPS)
    return (y * g.astype(F32)).astype(x.dtype)


def chunk_spatial_gate(v, w_s, b_s):
    bsz, L, _ = v.shape
    c = min(CHUNK, L)
    n = -(-L // c)
    pad = n * c - L
    if pad:
        v = jnp.pad(v, ((0, 0), (0, pad), (0, 0)))
    vg = v.reshape(bsz, n, c, A_GROUPS, A_GROUP_DIM)
    w = jnp.tril(w_s[:, :c, :c])
    mixed = jnp.einsum('hts,bnshg->bnthg', w, vg) + b_s[:, :c].T[None, None, :, :, None]
    return mixed.reshape(bsz, n * c, A_WIDTH)[:, :L]


def chunk_mlp_mixer(h, w_in, v_gain, w_s, b_s, w_out):
    z = jax.nn.gelu(h @ w_in)
    u, v = jnp.split(z, 2, axis=-1)
    v = rmsnorm(v, v_gain)
    y = (u * chunk_spatial_gate(v, w_s, b_s)) @ w_out
    return y, v


def short_conv_mixer(h, buf, w_in, conv_w, w_out):
    L = h.shape[1]
    b_gate, c_gate, hin = jnp.split(h @ w_in, 3, axis=-1)
    z = jnp.concatenate([buf.astype(h.dtype), c_gate * hin], axis=1)
    conv = conv_w[0] * z[:, 0:L]
    for k in range(1, CONV_W):
        conv = conv + conv_w[k] * z[:, k:k + L]
    y = (b_gate * conv) @ w_out
    return y, z[:, -(CONV_W - 1):]


def multiscale_pool_mixer(h, buf, pos0, w_group, scale):
    bsz, L, _ = h.shape
    hc = jnp.concatenate([buf.astype(h.dtype), h], axis=1)
    cs = jnp.cumsum(hc.astype(F32), axis=1)
    cs = jnp.pad(cs, ((0, 0), (1, 0), (0, 0)))
    end = cs[:, POOL_BUF + 1:POOL_BUF + 1 + L]
    pos = pos0 + jnp.arange(L)
    pooled = []
    for g, w in enumerate(POOL_WINDOWS):
        sl = slice(g * POOL_GROUP_DIM, (g + 1) * POOL_GROUP_DIM)
        start = cs[:, POOL_BUF + 1 - w:POOL_BUF + 1 - w + L, sl]
        count = jnp.minimum(w, pos + 1).astype(F32)
        pooled.append((end[..., sl] - start) / count[None, :, None])
    pooled = jnp.concatenate(pooled, axis=-1)
    diff = (pooled - h.astype(F32)).astype(h.dtype).reshape(bsz, L, POOL_GROUPS, POOL_GROUP_DIM)
    y = jnp.einsum('blgc,gcd->blgd', diff, w_group).reshape(bsz, L, D_MODEL) * scale
    return y, hc[:, -POOL_BUF:]


def swiglu(h, wg, wu, wd):
    return (jax.nn.silu(h @ wg) * (h @ wu)) @ wd


def moe_swiglu(h, router, wg, wu, wd):
    bsz, L, D = h.shape
    t = h.reshape(-1, D)
    logits = (t @ router).astype(F32)
    top_val, top_idx = lax.top_k(logits, TOP_K)
    gates = jax.nn.softmax(top_val, axis=-1)
    comb = jnp.sum(jax.nn.one_hot(top_idx, N_EXPERTS, dtype=F32) * gates[..., None], axis=1)
    out = jnp.zeros_like(t)
    for e in range(N_EXPERTS):
        out = out + comb[:, e:e + 1].astype(t.dtype) * swiglu(t, wg[e], wu[e], wd[e])
    return out.reshape(bsz, L, D)


def setup_inputs(seed: int = 0) -> dict:
    key = jax.random.key(seed)
    ks = iter(jax.random.split(key, 32))

    def nrm(shape, scale):
        return jax.random.normal(next(ks), shape, F32) * scale

    def gain(shape):
        return 1.0 + 0.05 * jax.random.normal(next(ks), shape, F32)

    D = D_MODEL
    return {
        'x_prompt': nrm((BATCH, SEQ, D), 1.0),
        'x_sample': nrm((DEC_BATCH, DEC_SEQ, D), 1.0),
        'state_conv': nrm((N_B, DEC_BATCH, CONV_W - 1, CONV_DIM), 1.0),
        'state_pool': nrm((N_C, DEC_BATCH, POOL_BUF, D), 1.0),
        'norm_mix': gain((DEPTH, D)),
        'norm_ffn': gain((DEPTH, D)),
        'final_norm': gain((D,)),
        'a_w_in': nrm((N_A, D, 2 * A_WIDTH), D ** -0.5),
        'a_v_gain': gain((N_A, A_WIDTH)),
        'a_w_s': nrm((N_A, A_GROUPS, CHUNK, CHUNK), CHUNK ** -0.5),
        'a_b_s': nrm((N_A, A_GROUPS, CHUNK), 0.02),
        'a_w_out': nrm((N_A, A_WIDTH, D), A_WIDTH ** -0.5),
        'b_w_in': nrm((N_B, D, 3 * CONV_DIM), D ** -0.5),
        'b_conv': nrm((N_B, CONV_W, CONV_DIM), CONV_W ** -0.5),
        'b_w_out': nrm((N_B, CONV_DIM, D), CONV_DIM ** -0.5),
        'c_w_group': nrm((N_C, POOL_GROUPS, POOL_GROUP_DIM, POOL_GROUP_DIM), POOL_GROUP_DIM ** -0.5),
        'c_scale': gain((N_C, D)),
        'ffn_w_gate': nrm((N_DENSE, D, D_FF), D ** -0.5),
        'ffn_w_up': nrm((N_DENSE, D, D_FF), D ** -0.5),
        'ffn_w_down': nrm((N_DENSE, D_FF, D), D_FF ** -0.5),
        'moe_router': nrm((N_MOE, D, N_EXPERTS), D ** -0.5),
        'moe_w_gate': nrm((N_MOE, N_EXPERTS, D, D_FF), D ** -0.5),
        'moe_w_up': nrm((N_MOE, N_EXPERTS, D, D_FF), D ** -0.5),
        'moe_w_down': nrm((N_MOE, N_EXPERTS, D_FF, D), D_FF ** -0.5),
    }


def reference(x_prompt, x_sample, state_conv, state_pool, norm_mix, norm_ffn, final_norm,
              a_w_in, a_v_gain, a_w_s, a_b_s, a_w_out, b_w_in, b_conv, b_w_out,
              c_w_group, c_scale, ffn_w_gate, ffn_w_up, ffn_w_down,
              moe_router, moe_w_gate, moe_w_up, moe_w_down):
    xp, xs = x_prompt, x_sample
    chunk_v_s, conv_p, conv_s, pool_p, pool_s = [], [], [], [], []
    for i in range(DEPTH):
        j = i // N_MIXERS
        kind = i % N_MIXERS
        hp = rmsnorm(xp, norm_mix[i])
        hs = rmsnorm(xs, norm_mix[i])
        if kind == 0:
            yp, _ = chunk_mlp_mixer(hp, a_w_in[j], a_v_gain[j], a_w_s[j], a_b_s[j], a_w_out[j])
            ys, vs = chunk_mlp_mixer(hs, a_w_in[j], a_v_gain[j], a_w_s[j], a_b_s[j], a_w_out[j])
            chunk_v_s.append(vs)
        elif kind == 1:
            zero = jnp.zeros((xp.shape[0], CONV_W - 1, CONV_DIM), xp.dtype)
            yp, bp = short_conv_mixer(hp, zero, b_w_in[j], b_conv[j], b_w_out[j])
            ys, bs = short_conv_mixer(hs, state_conv[j], b_w_in[j], b_conv[j], b_w_out[j])
            conv_p.append(bp)
            conv_s.append(bs)
        else:
            zero = jnp.zeros((xp.shape[0], POOL_BUF, D_MODEL), xp.dtype)
            yp, bp = multiscale_pool_mixer(hp, zero, 0, c_w_group[j], c_scale[j])
            ys, bs = multiscale_pool_mixer(hs, state_pool[j], PAST_LEN, c_w_group[j], c_scale[j])
            pool_p.append(bp)
            pool_s.append(bs)
        xp = xp + yp
        xs = xs + ys
        hp = rmsnorm(xp, norm_ffn[i])
        hs = rmsnorm(xs, norm_ffn[i])
        f = i // 2
        if i % 2 == 0:
            xp = xp + swiglu(hp, ffn_w_gate[f], ffn_w_up[f], ffn_w_down[f])
            xs = xs + swiglu(hs, ffn_w_gate[f], ffn_w_up[f], ffn_w_down[f])
        else:
            xp = xp + moe_swiglu(hp, moe_router[f], moe_w_gate[f], moe_w_up[f], moe_w_down[f])
            xs = xs + moe_swiglu(hs, moe_router[f], moe_w_gate[f], moe_w_up[f], moe_w_down[f])
    y_prompt = rmsnorm(xp, final_norm)
    y_sample = rmsnorm(xs, final_norm)
    return (y_prompt, y_sample, jnp.stack(chunk_v_s), jnp.stack(conv_p), jnp.stack(conv_s),
            jnp.stack(pool_p), jnp.stack(pool_s))
```

```python
import functools
import math

import jax
import jax.numpy as jnp
from jax import lax
from jax.experimental import pallas as pl
from jax.experimental.pallas import tpu as pltpu

F32 = jnp.float32
BF16 = jnp.bfloat16
EPS = 1e-6
PAST_LEN = 16384
POOL_WINDOWS = (2, 4, 8, 16)
TOP_K = 2

V7X_VMEM_BYTES = 64 * 1024 * 1024
SUBLANES = 8
LANES = 128

TM = 512
TN = 512
FFN_ROWS = 1024
FFN_SUB = 256
FFN_TF = 256
ROUTE_TM = 256


def _vmem_limit(nbytes):
    return min(int(nbytes) + (6 << 20), V7X_VMEM_BYTES - (4 << 20))


def _rms(x, gain):
    ms = jnp.mean(x * x, axis=-1, keepdims=True)
    return x * lax.rsqrt(ms + EPS) * gain


def _gelu_tanh(x):
    c = math.sqrt(2.0 / math.pi)
    return 0.5 * x * (1.0 + jnp.tanh(c * (x + 0.044715 * (x * x * x))))


def _bdot(a, b):
    return jnp.dot(a, b, preferred_element_type=F32)


def _inproj_gelu_kernel(x_ref, g_ref, w_ref, o_ref, h_scr):
    @pl.when(pl.program_id(1) == 0)
    def _():
        h_scr[...] = _rms(x_ref[...], g_ref[...]).astype(BF16)

    z = _bdot(h_scr[...], w_ref[0].astype(BF16))
    o_ref[...] = _gelu_tanh(z)


def _inproj_gelu(x, gain, w, layer):
    m, d = x.shape
    n = w.shape[2]
    vm = 2 * TM * d * 4 + TM * d * 2 + 2 * d * TN * 4 + d * TN * 2 + 4 * TM * TN * 4
    return pl.pallas_call(
        _inproj_gelu_kernel,
        grid=(m // TM, n // TN),
        in_specs=[
            pl.BlockSpec((TM, d), lambda i, j: (i, 0)),
            pl.BlockSpec((1, d), lambda i, j: (0, 0)),
            pl.BlockSpec((1, d, TN), lambda i, j: (layer, 0, j)),
        ],
        out_specs=pl.BlockSpec((TM, TN), lambda i, j: (i, j)),
        out_shape=jax.ShapeDtypeStruct((m, n), F32),
        scratch_shapes=[pltpu.VMEM((TM, d), BF16)],
        compiler_params=pltpu.CompilerParams(
            dimension_semantics=("arbitrary", "arbitrary"),
            vmem_limit_bytes=_vmem_limit(vm)),
        name="inproj_gelu",
    )(x, gain.reshape(1, d), w)


def _inproj_conv_kernel(x_ref, g_ref, wb_ref, wc_ref, wh_ref, ob_ref, oz_ref, h_scr):
    @pl.when(pl.program_id(1) == 0)
    def _():
        h_scr[...] = _rms(x_ref[...], g_ref[...]).astype(BF16)

    h = h_scr[...]
    ob_ref[...] = _bdot(h, wb_ref[0].astype(BF16))
    c = _bdot(h, wc_ref[0].astype(BF16))
    hin = _bdot(h, wh_ref[0].astype(BF16))
    oz_ref[...] = c * hin


def _inproj_conv(x, gain, w, layer):
    m, d = x.shape
    cd = w.shape[2] // 3
    nb = cd // TN
    vm = 2 * TM * d * 4 + TM * d * 2 + 3 * (2 * d * TN * 4 + d * TN * 2) + 6 * TM * TN * 4
    out = jax.ShapeDtypeStruct((m, cd), F32)
    return pl.pallas_call(
        _inproj_conv_kernel,
        grid=(m // TM, nb),
        in_specs=[
            pl.BlockSpec((TM, d), lambda i, j: (i, 0)),
            pl.BlockSpec((1, d), lambda i, j: (0, 0)),
            pl.BlockSpec((1, d, TN), lambda i, j: (layer, 0, j)),
            pl.BlockSpec((1, d, TN), lambda i, j: (layer, 0, j + nb)),
            pl.BlockSpec((1, d, TN), lambda i, j: (layer, 0, j + 2 * nb)),
        ],
        out_specs=[pl.BlockSpec((TM, TN), lambda i, j: (i, j)),
                   pl.BlockSpec((TM, TN), lambda i, j: (i, j))],
        out_shape=[out, out],
        scratch_shapes=[pltpu.VMEM((TM, d), BF16)],
        compiler_params=pltpu.CompilerParams(
            dimension_semantics=("arbitrary", "arbitrary"),
            vmem_limit_bytes=_vmem_limit(vm)),
        name="inproj_conv",
    )(x, gain.reshape(1, d), w, w, w)


def _chunk_out_kernel(u_ref, v_ref, vg_ref, gw_ref, gb_ref, sw_ref, sb_ref, w_ref, x_ref,
                      o_ref, cv_ref, p_scr, *, np_tiles, chunk, groups, slab, steps):
    i = pl.program_id(0)
    j = pl.program_id(1)
    gd = p_scr.shape[1] // groups

    @pl.when(jnp.logical_and(j == 0, i < np_tiles))
    def _prompt():
        row = lax.broadcasted_iota(jnp.int32, (chunk, chunk), 0)
        col = lax.broadcasted_iota(jnp.int32, (chunk, chunk), 1)
        causal = col <= row
        for s in range(p_scr.shape[0] // chunk):
            rs = slice(s * chunk, (s + 1) * chunk)
            vb = _rms(v_ref[rs, :], vg_ref[...]).astype(BF16)
            for h in range(groups):
                cs = slice(h * gd, (h + 1) * gd)
                wm = jnp.where(causal, gw_ref[h], 0.0).astype(BF16)
                mixed = _bdot(wm, vb[:, cs]) + gb_ref[:, h:h + 1]
                p_scr[rs, cs] = (u_ref[rs, cs] * mixed).astype(BF16)

    @pl.when(jnp.logical_and(j == 0, i >= np_tiles))
    def _sample():
        for t in range(steps):
            rs = slice(t * slab, (t + 1) * slab)
            cv_ref[rs, :] = _rms(v_ref[rs, :], vg_ref[...])
        for t in range(steps):
            rs = slice(t * slab, (t + 1) * slab)
            for h in range(groups):
                cs = slice(h * gd, (h + 1) * gd)
                mixed = sw_ref[t * steps:t * steps + 1, cs] * cv_ref[0:slab, cs]
                for s in range(1, t + 1):
                    k = t * steps + s
                    mixed = mixed + sw_ref[k:k + 1, cs] * cv_ref[s * slab:(s + 1) * slab, cs]
                mixed = mixed + sb_ref[t:t + 1, cs]
                p_scr[rs, cs] = (u_ref[rs, cs] * mixed).astype(BF16)

    o_ref[...] = x_ref[...] + _bdot(p_scr[...], w_ref[0].astype(BF16))


def _chunk_out(z, x, v_gain, w_s, b_s, w_out, layer, *, m_p, slab, steps):
    m, d = x.shape
    a = z.shape[1] // 2
    groups, chunk = w_s.shape[0], w_s.shape[1]
    gd = a // groups
    assert m_p % TM == 0 and TM % chunk == 0 and m - m_p == TM == slab * steps
    assert steps <= chunk and slab % SUBLANES == 0 and a % TN == 0 and d % TN == 0
    np_tiles = m_p // TM
    na = a // TN
    gb = b_s.T
    sw = jnp.repeat(w_s[:, :steps, :steps].transpose(1, 2, 0).reshape(steps * steps, groups), gd, axis=1)
    sb = jnp.repeat(b_s[:, :steps].T, gd, axis=1)
    vm = (4 * TM * a * 4 + TM * a * 2 + 2 * groups * chunk * chunk * 4 + 2 * d * TN * 4 + d * TN * 2
          + 4 * TM * TN * 4 + 2 * TM * a * 4 + 4 * (steps * steps + steps) * a * 4)
    kern = functools.partial(_chunk_out_kernel, np_tiles=np_tiles, chunk=chunk, groups=groups,
                             slab=slab, steps=steps)
    return pl.pallas_call(
        kern,
        grid=(m // TM, d // TN),
        in_specs=[
            pl.BlockSpec((TM, a), lambda i, j: (i, 0)),
            pl.BlockSpec((TM, a), lambda i, j: (i, 1)),
            pl.BlockSpec((1, a), lambda i, j: (0, 0)),
            pl.BlockSpec((groups, chunk, chunk), lambda i, j: (0, 0, 0)),
            pl.BlockSpec((chunk, groups), lambda i, j: (0, 0)),
            pl.BlockSpec((steps * steps, a), lambda i, j: (0, 0)),
            pl.BlockSpec((steps, a), lambda i, j: (0, 0)),
            pl.BlockSpec((1, a, TN), lambda i, j: (layer, 0, j)),
            pl.BlockSpec((TM, TN), lambda i, j: (i, j)),
        ],
        out_specs=[pl.BlockSpec((TM, TN), lambda i, j: (i, j)),
                   pl.BlockSpec((TM, a), lambda i, j: (0, 0))],
        out_shape=[jax.ShapeDtypeStruct((m, d), F32), jax.ShapeDtypeStruct((TM, a), F32)],
        scratch_shapes=[pltpu.VMEM((TM, a), BF16)],
        compiler_params=pltpu.CompilerParams(
            dimension_semantics=("arbitrary", "arbitrary"),
            vmem_limit_bytes=_vmem_limit(vm)),
        name="chunk_out",
    )(z, z, v_gain.reshape(1, a), w_s, gb, sw, sb, w_out, x)


def _conv_out_kernel(bg_ref, z_ref, halo_ref, st_ref, cw_ref, w_ref, x_ref, o_ref, p_scr,
                     *, np_tiles, tiles_per_seq, slab, steps):
    i = pl.program_id(0)
    j = pl.program_id(1)
    width = cw_ref.shape[0]
    nhalo = halo_ref.shape[0]
    tm, cdim = p_scr.shape

    @pl.when(jnp.logical_and(j == 0, i < np_tiles))
    def _prompt():
        keep = (i % tiles_per_seq != 0).astype(F32)
        for c in range(cdim // TN):
            cs = slice(c * TN, (c + 1) * TN)
            ext = jnp.concatenate([halo_ref[:, cs] * keep, z_ref[:, cs]], axis=0)
            conv = cw_ref[0:1, cs] * pltpu.roll(ext, width - 1, axis=0)[nhalo:]
            for k in range(1, width):
                sh = width - 1 - k
                zk = pltpu.roll(ext, sh, axis=0)[nhalo:] if sh else z_ref[:, cs]
                conv = conv + cw_ref[k:k + 1, cs] * zk
            p_scr[:, cs] = (bg_ref[:, cs] * conv).astype(BF16)

    @pl.when(jnp.logical_and(j == 0, i >= np_tiles))
    def _sample():
        hist = width - 1

        def zrow(r, cs):
            if r < hist:
                return st_ref[r * slab:(r + 1) * slab, cs]
            return z_ref[(r - hist) * slab:(r - hist + 1) * slab, cs]

        for c in range(cdim // TN):
            cs = slice(c * TN, (c + 1) * TN)
            for t in range(steps):
                conv = cw_ref[0:1, cs] * zrow(t, cs)
                for k in range(1, width):
                    conv = conv + cw_ref[k:k + 1, cs] * zrow(t + k, cs)
                rs = slice(t * slab, (t + 1) * slab)
                p_scr[rs, cs] = (bg_ref[rs, cs] * conv).astype(BF16)

    o_ref[...] = x_ref[...] + _bdot(p_scr[...], w_ref[0].astype(BF16))


def _conv_out(bg, zc, x, state_t, conv_w, w_out, layer, *, m_p, seq, slab, steps):
    m, d = x.shape
    cd = zc.shape[1]
    width = conv_w.shape[0]
    assert m - m_p == TM == slab * steps and seq % TM == 0 and width - 1 <= SUBLANES
    np_tiles = m_p // TM
    halo_blocks = TM // SUBLANES
    vm = (4 * TM * cd * 4 + 2 * SUBLANES * cd * 4 + 2 * (width - 1) * slab * cd * 4 + TM * cd * 2
          + 2 * cd * TN * 4 + cd * TN * 2 + 4 * TM * TN * 4 + 6 * TM * TN * 4)
    kern = functools.partial(_conv_out_kernel, np_tiles=np_tiles, tiles_per_seq=seq // TM,
                             slab=slab, steps=steps)
    return pl.pallas_call(
        kern,
        grid=(m // TM, d // TN),
        in_specs=[
            pl.BlockSpec((TM, cd), lambda i, j: (i, 0)),
            pl.BlockSpec((TM, cd), lambda i, j: (i, 0)),
            pl.BlockSpec((SUBLANES, cd), lambda i, j: (jnp.maximum(i * halo_blocks - 1, 0), 0)),
            pl.BlockSpec(((width - 1) * slab, cd), lambda i, j: (0, 0)),
            pl.BlockSpec((width, cd), lambda i, j: (0, 0)),
            pl.BlockSpec((1, cd, TN), lambda i, j: (layer, 0, j)),
            pl.BlockSpec((TM, TN), lambda i, j: (i, j)),
        ],
        out_specs=pl.BlockSpec((TM, TN), lambda i, j: (i, j)),
        out_shape=jax.ShapeDtypeStruct((m, d), F32),
        scratch_shapes=[pltpu.VMEM((TM, cd), BF16)],
        compiler_params=pltpu.CompilerParams(
            dimension_semantics=("arbitrary", "arbitrary"),
            vmem_limit_bytes=_vmem_limit(vm)),
        name="conv_out",
    )(bg, zc, zc, state_t, conv_w, w_out, x)


def _pool_kernel(xf_ref, hf_ref, xg_ref, hg_ref, buf_ref, gf_ref, gg_ref, w_ref, sc_ref,
                 o_ref, hn_ref, r_scr, *, np_tiles, tiles_per_seq, slab, steps, windows, pos0):
    i = pl.program_id(0)
    g = pl.program_id(1)
    tm = xg_ref.shape[0]
    nhalo = hg_ref.shape[0]
    nbuf = buf_ref.shape[0] // slab

    @pl.when(g == 0)
    def _():
        xf = xf_ref[...]
        r_scr[nhalo:, :] = lax.rsqrt(jnp.mean(xf * xf, axis=-1, keepdims=True) + EPS)
        hf = hf_ref[...]
        r_scr[:nhalo, :] = lax.rsqrt(jnp.mean(hf * hf, axis=-1, keepdims=True) + EPS)

    h = xg_ref[...] * r_scr[nhalo:, :] * gg_ref[...]
    hn_ref[...] = h

    def finish(pooled):
        diff = (pooled - h).astype(BF16)
        y = _bdot(diff, w_ref[0].astype(BF16)) * sc_ref[...]
        o_ref[...] = xg_ref[...] + y

    for gi, win in enumerate(windows):
        @pl.when(jnp.logical_and(g == gi, i < np_tiles))
        def _prompt(win=win):
            seq_tile = i % tiles_per_seq
            keep = (seq_tile != 0).astype(F32)
            hh = hg_ref[...] * r_scr[:nhalo, :] * gg_ref[...] * keep
            s = jnp.concatenate([hh, h], axis=0)
            k = 1
            while k < win:
                s = s + pltpu.roll(s, k, axis=0)
                k *= 2
            pos = seq_tile * tm + lax.broadcasted_iota(jnp.int32, (tm, 1), 0)
            count = jnp.minimum(win, pos + 1).astype(F32)
            finish(s[nhalo:] / count)

        @pl.when(jnp.logical_and(g == gi, i >= np_tiles))
        def _sample(win=win):
            def hrow(r):
                if r < nbuf:
                    return buf_ref[r * slab:(r + 1) * slab, :]
                return h[(r - nbuf) * slab:(r - nbuf + 1) * slab]

            rows = []
            for t in range(steps):
                acc = hrow(nbuf + t - win + 1)
                for r in range(nbuf + t - win + 2, nbuf + t + 1):
                    acc = acc + hrow(r)
                rows.append(acc / float(min(win, pos0 + t + 1)))
            finish(jnp.concatenate(rows, axis=0))


def _pool_mixer(x, gain, buf_t, w_group, scale, *, m_p, seq, slab, steps):
    m, d = x.shape
    ng, gdim = w_group.shape[0], w_group.shape[1]
    nhalo = 2 * SUBLANES
    assert ng == len(POOL_WINDOWS) and max(POOL_WINDOWS) <= nhalo and gdim * ng == d
    assert m - m_p == TM == slab * steps and seq % TM == 0
    assert buf_t.shape[0] // slab >= max(POOL_WINDOWS) - 1
    np_tiles = m_p // TM
    halo_blocks = TM // nhalo
    nbuf_rows = buf_t.shape[0]
    vm = (2 * TM * d * 4 + 2 * nhalo * d * 4 + 2 * TM * gdim * 4 + 2 * nhalo * gdim * 4
          + 2 * nbuf_rows * gdim * 4 + 2 * gdim * gdim * 4 + gdim * gdim * 2
          + 4 * TM * gdim * 4 + (TM + nhalo) * LANES * 4 + 10 * TM * gdim * 4)
    kern = functools.partial(_pool_kernel, np_tiles=np_tiles, tiles_per_seq=seq // TM, slab=slab,
                             steps=steps, windows=POOL_WINDOWS, pos0=PAST_LEN)
    halo_idx = lambda i: jnp.maximum(i * halo_blocks - 1, 0)
    return pl.pallas_call(
        kern,
        grid=(m // TM, ng),
        in_specs=[
            pl.BlockSpec((TM, d), lambda i, g: (i, 0)),
            pl.BlockSpec((nhalo, d), lambda i, g: (halo_idx(i), 0)),
            pl.BlockSpec((TM, gdim), lambda i, g: (i, g)),
            pl.BlockSpec((nhalo, gdim), lambda i, g: (halo_idx(i), g)),
            pl.BlockSpec((nbuf_rows, gdim), lambda i, g: (0, g)),
            pl.BlockSpec((1, d), lambda i, g: (0, 0)),
            pl.BlockSpec((1, gdim), lambda i, g: (0, g)),
            pl.BlockSpec((1, gdim, gdim), lambda i, g: (g, 0, 0)),
            pl.BlockSpec((1, gdim), lambda i, g: (0, g)),
        ],
        out_specs=[pl.BlockSpec((TM, gdim), lambda i, g: (i, g)),
                   pl.BlockSpec((TM, gdim), lambda i, g: (i, g))],
        out_shape=[jax.ShapeDtypeStruct((m, d), F32), jax.ShapeDtypeStruct((m, d), F32)],
        scratch_shapes=[pltpu.VMEM((TM + nhalo, 1), F32)],
        compiler_params=pltpu.CompilerParams(
            dimension_semantics=("arbitrary", "arbitrary"),
            vmem_limit_bytes=_vmem_limit(vm)),
        name="pool_mixer",
    )(x, x, x, x, buf_t, gain.reshape(1, d), gain.reshape(1, d), w_group, scale.reshape(1, d))


def _ffn_kernel(ce_ref, ns_ref, nv_ref, x_ref, gain_ref, wg_ref, wu_ref, wd_ref, o_ref,
                xb_scr, wg_scr, wu_scr, wd_scr, *, dense):
    c = pl.program_id(0)
    f = pl.program_id(1)
    rows = x_ref.shape[0]
    nvalid = nv_ref[c]

    @pl.when(f == 0)
    def _init():
        for s in range(rows // FFN_SUB):
            rs = slice(s * FFN_SUB, (s + 1) * FFN_SUB)
            x = x_ref[rs, :]
            if dense:
                live = (s * FFN_SUB + lax.broadcasted_iota(jnp.int32, (FFN_SUB, 1), 0)) < nvalid
                o_ref[rs, :] = x
                x = jnp.where(live, _rms(x, gain_ref[...]), 0.0)
            else:
                o_ref[rs, :] = jnp.zeros_like(x)
            xb_scr[rs, :] = x.astype(BF16)

    wg_scr[...] = wg_ref[0].astype(BF16)
    wu_scr[...] = wu_ref[0].astype(BF16)
    wd_scr[...] = wd_ref[0].astype(BF16)

    def sub_tile(s, carry):
        rs = pl.ds(pl.multiple_of(s * FFN_SUB, FFN_SUB), FFN_SUB)
        xs = xb_scr[rs, :]
        gate = _bdot(xs, wg_scr[...])
        up = _bdot(xs, wu_scr[...])
        act = (gate * jax.nn.sigmoid(gate) * up).astype(BF16)
        o_ref[rs, :] += _bdot(act, wd_scr[...])
        return carry

    lax.fori_loop(0, ns_ref[c], sub_tile, 0)


def _ffn(x, gain, wg, wu, wd, chunk_expert, chunk_nsub, chunk_nvalid, *, dense):
    m, d = x.shape
    dff = wg.shape[-1]
    nch = chunk_expert.shape[0]
    nf = dff // FFN_TF
    assert dff % FFN_TF == 0 and FFN_ROWS % FFN_SUB == 0

    def w_in_map(c, f, ce, ns, nv):
        return (ce[c], 0, jnp.where(ns[c] > 0, f, nf - 1))

    def w_out_map(c, f, ce, ns, nv):
        return (ce[c], jnp.where(ns[c] > 0, f, nf - 1), 0)

    vm = (4 * FFN_ROWS * d * 4 + FFN_ROWS * d * 2 + 3 * (2 * d * FFN_TF * 4 + d * FFN_TF * 2)
          + 8 * FFN_SUB * FFN_TF * 4)
    grid_spec = pltpu.PrefetchScalarGridSpec(
        num_scalar_prefetch=3,
        grid=(nch, nf),
        in_specs=[
            pl.BlockSpec((FFN_ROWS, d), lambda c, f, ce, ns, nv: (c, 0)),
            pl.BlockSpec((1, d), lambda c, f, ce, ns, nv: (0, 0)),
            pl.BlockSpec((1, d, FFN_TF), w_in_map),
            pl.BlockSpec((1, d, FFN_TF), w_in_map),
            pl.BlockSpec((1, FFN_TF, d), w_out_map),
        ],
        out_specs=pl.BlockSpec((FFN_ROWS, d), lambda c, f, ce, ns, nv: (c, 0)),
        scratch_shapes=[pltpu.VMEM((FFN_ROWS, d), BF16), pltpu.VMEM((d, FFN_TF), BF16),
                        pltpu.VMEM((d, FFN_TF), BF16), pltpu.VMEM((FFN_TF, d), BF16)],
    )
    return pl.pallas_call(
        functools.partial(_ffn_kernel, dense=dense),
        grid_spec=grid_spec,
        out_shape=jax.ShapeDtypeStruct((m, d), F32),
        compiler_params=pltpu.CompilerParams(
            dimension_semantics=("arbitrary", "arbitrary"),
            vmem_limit_bytes=_vmem_limit(vm)),
        name="ffn_dense" if dense else "ffn_experts",
    )(chunk_expert, chunk_nsub, chunk_nvalid, x, gain.reshape(1, d), wg, wu, wd)


def _dense_ffn(x, gain, wg, wu, wd, layer):
    m = x.shape[0]
    nch = pl.cdiv(m, FFN_ROWS)
    nvalid = jnp.minimum(FFN_ROWS, m - FFN_ROWS * jnp.arange(nch, dtype=jnp.int32))
    nsub = (nvalid + FFN_SUB - 1) // FFN_SUB
    which = jnp.full((nch,), layer, jnp.int32)
    return _ffn(x, gain, wg, wu, wd, which, nsub, nvalid, dense=True)


def _route_kernel(x_ref, g_ref, r_ref, hn_ref, info_ref, cnt_ref, carry_scr, *, n_experts):
    i = pl.program_id(0)
    tm = x_ref.shape[0]

    @pl.when(i == 0)
    def _():
        carry_scr[...] = jnp.zeros_like(carry_scr)

    hn = _rms(x_ref[...], g_ref[...])
    hn_ref[...] = hn
    lane = lax.broadcasted_iota(jnp.int32, (tm, LANES), 1).astype(F32)
    logits = jnp.dot(hn, r_ref[...], preferred_element_type=F32, precision=lax.Precision.HIGHEST)
    logits = jnp.where(lane < n_experts, logits, -jnp.inf)
    m1 = jnp.max(logits, axis=-1, keepdims=True)
    i1 = jnp.min(jnp.where(logits == m1, lane, float(LANES)), axis=-1, keepdims=True)
    rest = jnp.where(lane == i1, -jnp.inf, logits)
    m2 = jnp.max(rest, axis=-1, keepdims=True)
    i2 = jnp.min(jnp.where(rest == m2, lane, float(LANES)), axis=-1, keepdims=True)
    e = jnp.exp(m2 - m1)
    g1 = 1.0 / (1.0 + e)
    g2 = e / (1.0 + e)
    oh1 = (lane == i1).astype(F32)
    oh2 = (lane == i2).astype(F32)
    cnt = oh1 + oh2
    row = lax.broadcasted_iota(jnp.int32, (tm, tm), 0)
    col = lax.broadcasted_iota(jnp.int32, (tm, tm), 1)
    before = (col < row).astype(BF16)
    ranks = _bdot(before, cnt.astype(BF16)) + carry_scr[0:1, :]
    rank1 = jnp.sum(ranks * oh1, axis=-1, keepdims=True)
    rank2 = jnp.sum(ranks * oh2, axis=-1, keepdims=True)
    carry_scr[0:1, :] = carry_scr[0:1, :] + jnp.sum(cnt, axis=0, keepdims=True)
    cnt_ref[...] = jnp.broadcast_to(carry_scr[0:1, :], cnt_ref.shape)
    info = jnp.where(lane == 0, i1, 0.0)
    info = jnp.where(lane == 1, i2, info)
    info = jnp.where(lane == 2, g1, info)
    info = jnp.where(lane == 3, g2, info)
    info = jnp.where(lane == 4, rank1, info)
    info = jnp.where(lane == 5, rank2, info)
    info_ref[...] = info


def _route(x, gain, router):
    m, d = x.shape
    ne = router.shape[1]
    rpad = jnp.pad(router, ((0, 0), (0, LANES - ne)))
    vm = 4 * TM * d * 4 + 2 * d * LANES * 4 + 4 * TM * LANES * 4 + TM * TM * 8 + 8 * TM * d * 4
    return pl.pallas_call(
        functools.partial(_route_kernel, n_experts=ne),
        grid=(m // TM,),
        in_specs=[
            pl.BlockSpec((TM, d), lambda i: (i, 0)),
            pl.BlockSpec((1, d), lambda i: (0, 0)),
            pl.BlockSpec((d, LANES), lambda i: (0, 0)),
        ],
        out_specs=[pl.BlockSpec((TM, d), lambda i: (i, 0)),
                   pl.BlockSpec((TM, LANES), lambda i: (i, 0)),
                   pl.BlockSpec((SUBLANES, LANES), lambda i: (0, 0))],
        out_shape=[jax.ShapeDtypeStruct((m, d), F32), jax.ShapeDtypeStruct((m, LANES), F32),
                   jax.ShapeDtypeStruct((SUBLANES, LANES), F32)],
        scratch_shapes=[pltpu.VMEM((SUBLANES, LANES), F32)],
        compiler_params=pltpu.CompilerParams(
            dimension_semantics=("arbitrary",), vmem_limit_bytes=_vmem_limit(vm)),
        name="route",
    )(x, gain.reshape(1, d), rpad)


def _row_copy(src, src_row, dst, dst_row, sem):
    return pltpu.make_async_copy(src.at[pl.ds(src_row, 1), :], dst.at[pl.ds(dst_row, 1), :], sem)


def _dispatch_kernel(p1_ref, p2_ref, hn_ref, init_ref, xs_ref, sem):
    del init_ref
    base = pl.program_id(0) * hn_ref.shape[0]
    tm = hn_ref.shape[0]

    def start(r, carry):
        _row_copy(hn_ref, r, xs_ref, p1_ref[base + r], sem).start()
        _row_copy(hn_ref, r, xs_ref, p2_ref[base + r], sem).start()
        return carry

    def wait(r, carry):
        _row_copy(hn_ref, r, xs_ref, p1_ref[base + r], sem).wait()
        _row_copy(hn_ref, r, xs_ref, p2_ref[base + r], sem).wait()
        return carry

    lax.fori_loop(0, tm, start, 0)
    lax.fori_loop(0, tm, wait, 0)


def _dispatch(hn, pos1, pos2, n_rows):
    m, d = hn.shape
    grid_spec = pltpu.PrefetchScalarGridSpec(
        num_scalar_prefetch=2,
        grid=(m // ROUTE_TM,),
        in_specs=[pl.BlockSpec((ROUTE_TM, d), lambda i, p1, p2: (i, 0)),
                  pl.BlockSpec(memory_space=pl.ANY)],
        out_specs=pl.BlockSpec(memory_space=pl.ANY),
        scratch_shapes=[pltpu.SemaphoreType.DMA(())],
    )
    return pl.pallas_call(
        _dispatch_kernel,
        grid_spec=grid_spec,
        out_shape=jax.ShapeDtypeStruct((n_rows, d), F32),
        input_output_aliases={3: 0},
        compiler_params=pltpu.CompilerParams(dimension_semantics=("arbitrary",)),
        name="dispatch",
    )(pos1, pos2, hn, jnp.zeros((n_rows, d), F32))


def _combine_kernel(p1_ref, p2_ref, x_ref, info_ref, gain_ref, ys_ref, o_ref, b1_scr, b2_scr, sem,
                    *, final_norm):
    tm = x_ref.shape[0]
    base = pl.program_id(0) * tm

    def start(r, carry):
        _row_copy(ys_ref, p1_ref[base + r], b1_scr, r, sem).start()
        _row_copy(ys_ref, p2_ref[base + r], b2_scr, r, sem).start()
        return carry

    def wait(r, carry):
        _row_copy(ys_ref, p1_ref[base + r], b1_scr, r, sem).wait()
        _row_copy(ys_ref, p2_ref[base + r], b2_scr, r, sem).wait()
        return carry

    lax.fori_loop(0, tm, start, 0)
    lax.fori_loop(0, tm, wait, 0)
    info = info_ref[...]
    e1, e2 = info[:, 0:1], info[:, 1:2]
    g1, g2 = info[:, 2:3], info[:, 3:4]
    y1 = g1 * b1_scr[...]
    y2 = g2 * b2_scr[...]
    lo = jnp.where(e1 < e2, y1, y2)
    hi = jnp.where(e1 < e2, y2, y1)
    out = x_ref[...] + (lo + hi)
    if final_norm:
        out = _rms(out, gain_ref[...])
    o_ref[...] = out


def _combine(x, info, ys, pos1, pos2, gain, *, final_norm):
    m, d = x.shape
    vm = 4 * ROUTE_TM * d * 4 + 2 * ROUTE_TM * LANES * 4 + 2 * ROUTE_TM * d * 4 + 6 * ROUTE_TM * d * 4
    grid_spec = pltpu.PrefetchScalarGridSpec(
        num_scalar_prefetch=2,
        grid=(m // ROUTE_TM,),
        in_specs=[
            pl.BlockSpec((ROUTE_TM, d), lambda i, p1, p2: (i, 0)),
            pl.BlockSpec((ROUTE_TM, LANES), lambda i, p1, p2: (i, 0)),
            pl.BlockSpec((1, d), lambda i, p1, p2: (0, 0)),
            pl.BlockSpec(memory_space=pl.ANY),
        ],
        out_specs=pl.BlockSpec((ROUTE_TM, d), lambda i, p1, p2: (i, 0)),
        scratch_shapes=[pltpu.VMEM((ROUTE_TM, d), F32), pltpu.VMEM((ROUTE_TM, d), F32),
                        pltpu.SemaphoreType.DMA(())],
    )
    return pl.pallas_call(
        functools.partial(_combine_kernel, final_norm=final_norm),
        grid_spec=grid_spec,
        out_shape=jax.ShapeDtypeStruct((m, d), F32),
        compiler_params=pltpu.CompilerParams(
            dimension_semantics=("arbitrary",), vmem_limit_bytes=_vmem_limit(vm)),
        name="combine_final" if final_norm else "combine",
    )(pos1, pos2, x, info, gain.reshape(1, d), ys)


def _moe_ffn(x, gain, router, wg, wu, wd, layer, out_gain, *, final_norm):
    m, d = x.shape
    ne = router.shape[1]
    wg, wu, wd = (w.reshape((-1,) + w.shape[2:]) for w in (wg, wu, wd))
    assert m % TM == 0 and m % ROUTE_TM == 0
    hn, info, cnt = _route(x, gain, router)
    counts = cnt[0, :ne].astype(jnp.int32)
    chunks_per = (counts + FFN_ROWS - 1) // FFN_ROWS
    chunk_end = jnp.cumsum(chunks_per)
    offsets = (chunk_end - chunks_per) * FFN_ROWS
    e1 = info[:, 0].astype(jnp.int32)
    e2 = info[:, 1].astype(jnp.int32)
    pos1 = offsets[e1] + info[:, 4].astype(jnp.int32)
    pos2 = offsets[e2] + info[:, 5].astype(jnp.int32)
    nch = (m * TOP_K) // FFN_ROWS + ne
    cidx = jnp.arange(nch, dtype=jnp.int32)
    last = jnp.maximum(chunk_end[-1] - 1, 0)
    owner = jnp.searchsorted(chunk_end, jnp.minimum(cidx, last), side="right").astype(jnp.int32)
    owner = jnp.minimum(owner, ne - 1)
    local = cidx - (chunk_end - chunks_per)[owner]
    nvalid = jnp.clip(counts[owner] - local * FFN_ROWS, 0, FFN_ROWS)
    nvalid = jnp.where(cidx < chunk_end[-1], nvalid, 0).astype(jnp.int32)
    nsub = (nvalid + FFN_SUB - 1) // FFN_SUB
    xs = _dispatch(hn, pos1, pos2, nch * FFN_ROWS)
    ys = _ffn(xs, gain, wg, wu, wd, owner + layer * ne, nsub, nvalid, dense=False)
    return _combine(x, info, ys, pos1, pos2, out_gain, final_norm=final_norm)


def kernel(x_prompt, x_sample, state_conv, state_pool, norm_mix, norm_ffn, final_norm, a_w_in, a_v_gain, a_w_s, a_b_s, a_w_out, b_w_in, b_conv, b_w_out, c_w_group, c_scale, ffn_w_gate, ffn_w_up, ffn_w_down, moe_router, moe_w_gate, moe_w_up, moe_w_down):
    batch, seq, d = x_prompt.shape
    slab, steps, _ = x_sample.shape
    depth = norm_mix.shape[0]
    assert depth % 2 == 0, "the final RMSNorm is fused into the routed combine kernel of the last layer"
    m_p = batch * seq
    geo = dict(m_p=m_p, slab=slab, steps=steps)

    def to_time_major(a):
        return a.transpose(1, 0, 2).reshape(a.shape[1] * slab, a.shape[2])

    def from_time_major(a, r):
        return a.reshape(r, slab, a.shape[-1]).transpose(1, 0, 2)

    x = jnp.concatenate([x_prompt.reshape(m_p, d), to_time_major(x_sample)], axis=0)
    chunk_v_s, conv_p, conv_s, pool_p, pool_s = [], [], [], [], []
    for i in range(depth):
        j, kind = divmod(i, 3)
        if kind == 0:
            z = _inproj_gelu(x, norm_mix[i], a_w_in, j)
            x, cv = _chunk_out(z, x, a_v_gain[j], a_w_s[j], a_b_s[j], a_w_out, j, **geo)
            chunk_v_s.append(from_time_major(cv, steps))
        elif kind == 1:
            hist = b_conv.shape[1] - 1
            bg, zc = _inproj_conv(x, norm_mix[i], b_w_in, j)
            x = _conv_out(bg, zc, x, to_time_major(state_conv[j]), b_conv[j], b_w_out, j, seq=seq, **geo)
            conv_p.append(zc[:m_p].reshape(batch, seq, -1)[:, seq - hist:])
            conv_s.append(from_time_major(zc[m_p:], steps)[:, steps - hist:])
        else:
            nbuf = state_pool.shape[2]
            x, hn = _pool_mixer(x, norm_mix[i], to_time_major(state_pool[j]), c_w_group[j], c_scale[j],
                                seq=seq, **geo)
            pool_p.append(hn[:m_p].reshape(batch, seq, d)[:, seq - nbuf:])
            hc = jnp.concatenate([state_pool[j], from_time_major(hn[m_p:], steps)], axis=1)
            pool_s.append(hc[:, hc.shape[1] - nbuf:])
        f = i // 2
        if i % 2 == 0:
            x = _dense_ffn(x, norm_ffn[i], ffn_w_gate, ffn_w_up, ffn_w_down, f)
        else:
            x = _moe_ffn(x, norm_ffn[i], moe_router[f], moe_w_gate, moe_w_up, moe_w_down, f,
                         final_norm, final_norm=(i == depth - 1))
    y_prompt = x[:m_p].reshape(batch, seq, d)
    y_sample = from_time_major(x[m_p:], steps)
    return (y_prompt, y_sample, jnp.stack(chunk_v_s), jnp.stack(conv_p), jnp.stack(conv_s),
            jnp.stack(pool_p), jnp.stack(pool_s))
```

```python
import functools
import math

import jax
import jax.numpy as jnp
from jax import lax
from jax.experimental import pallas as pl
from jax.experimental.pallas import tpu as pltpu

F32 = jnp.float32
BF16 = jnp.bfloat16
EPS = 1e-6
PAST_LEN = 16384
POOL_WINDOWS = (2, 4, 8, 16)
TOP_K = 2

V7X_VMEM_BYTES = 64 * 1024 * 1024
SUBLANES = 8
LANES = 128

TM = 512
TN = 512
FFN_ROWS = 1024
FFN_SUB = 256
FFN_TF = 256
ROUTE_TM = 256


def _vmem_limit(nbytes):
    return min(int(nbytes) + (6 << 20), V7X_VMEM_BYTES - (4 << 20))


def _rms(x, gain):
    ms = jnp.mean(x * x, axis=-1, keepdims=True)
    return x * lax.rsqrt(ms + EPS) * gain


def _gelu_tanh(x):
    c = math.sqrt(2.0 / math.pi)
    return 0.5 * x * (1.0 + jnp.tanh(c * (x + 0.044715 * (x * x * x))))


def _bdot(a, b):
    return jnp.dot(a, b, preferred_element_type=F32)


INPROJ_ROW_TILES = 8


def _inproj_rows(m):
    assert m % (INPROJ_ROW_TILES * 2 * SUBLANES) == 0
    return m // INPROJ_ROW_TILES


def _inproj_gelu_kernel(x_ref, g_ref, w_ref, o_ref, h_scr):
    @pl.when(pl.program_id(1) == 0)
    def _():
        h_scr[...] = _rms(x_ref[...], g_ref[...]).astype(BF16)

    z = _bdot(h_scr[...], w_ref[0].astype(BF16))
    o_ref[...] = _gelu_tanh(z)


def _inproj_gelu(x, gain, w, layer):
    m, d = x.shape
    n = w.shape[2]
    tm = _inproj_rows(m)
    vm = 2 * tm * d * 4 + tm * d * 2 + 2 * d * TN * 4 + d * TN * 2 + 4 * tm * TN * 4
    return pl.pallas_call(
        _inproj_gelu_kernel,
        grid=(m // tm, n // TN),
        in_specs=[
            pl.BlockSpec((tm, d), lambda i, j: (i, 0)),
            pl.BlockSpec((1, d), lambda i, j: (0, 0)),
            pl.BlockSpec((1, d, TN), lambda i, j: (layer, 0, j)),
        ],
        out_specs=pl.BlockSpec((tm, TN), lambda i, j: (i, j)),
        out_shape=jax.ShapeDtypeStruct((m, n), F32),
        scratch_shapes=[pltpu.VMEM((tm, d), BF16)],
        compiler_params=pltpu.CompilerParams(
            dimension_semantics=("arbitrary", "arbitrary"),
            vmem_limit_bytes=_vmem_limit(vm)),
        name="inproj_gelu",
    )(x, gain.reshape(1, d), w)


def _inproj_conv_kernel(x_ref, g_ref, wb_ref, wc_ref, wh_ref, ob_ref, oz_ref, h_scr):
    @pl.when(pl.program_id(1) == 0)
    def _():
        h_scr[...] = _rms(x_ref[...], g_ref[...]).astype(BF16)

    h = h_scr[...]
    ob_ref[...] = _bdot(h, wb_ref[0].astype(BF16))
    c = _bdot(h, wc_ref[0].astype(BF16))
    hin = _bdot(h, wh_ref[0].astype(BF16))
    oz_ref[...] = c * hin


def _inproj_conv(x, gain, w, layer):
    m, d = x.shape
    cd = w.shape[2] // 3
    tm = _inproj_rows(m)
    tn = TN // 2
    nb = cd // tn
    vm = 2 * tm * d * 4 + tm * d * 2 + 3 * (2 * d * tn * 4 + d * tn * 2) + 10 * tm * tn * 4
    out = jax.ShapeDtypeStruct((m, cd), F32)
    return pl.pallas_call(
        _inproj_conv_kernel,
        grid=(m // tm, nb),
        in_specs=[
            pl.BlockSpec((tm, d), lambda i, j: (i, 0)),
            pl.BlockSpec((1, d), lambda i, j: (0, 0)),
            pl.BlockSpec((1, d, tn), lambda i, j: (layer, 0, j)),
            pl.BlockSpec((1, d, tn), lambda i, j: (layer, 0, j + nb)),
            pl.BlockSpec((1, d, tn), lambda i, j: (layer, 0, j + 2 * nb)),
        ],
        out_specs=[pl.BlockSpec((tm, tn), lambda i, j: (i, j)),
                   pl.BlockSpec((tm, tn), lambda i, j: (i, j))],
        out_shape=[out, out],
        scratch_shapes=[pltpu.VMEM((tm, d), BF16)],
        compiler_params=pltpu.CompilerParams(
            dimension_semantics=("arbitrary", "arbitrary"),
            vmem_limit_bytes=_vmem_limit(vm)),
        name="inproj_conv",
    )(x, gain.reshape(1, d), w, w, w)


def _chunk_out_kernel(u_ref, v_ref, vg_ref, gw_ref, gb_ref, sw_ref, sb_ref, w_ref, x_ref,
                      o_ref, cv_ref, p_scr, *, np_tiles, chunk, groups, slab, steps):
    i = pl.program_id(0)
    j = pl.program_id(1)
    gd = p_scr.shape[1] // groups

    @pl.when(jnp.logical_and(j == 0, i < np_tiles))
    def _prompt():
        row = lax.broadcasted_iota(jnp.int32, (chunk, chunk), 0)
        col = lax.broadcasted_iota(jnp.int32, (chunk, chunk), 1)
        causal = col <= row
        for s in range(p_scr.shape[0] // chunk):
            rs = slice(s * chunk, (s + 1) * chunk)
            vb = _rms(v_ref[rs, :], vg_ref[...]).astype(BF16)
            for h in range(groups):
                cs = slice(h * gd, (h + 1) * gd)
                wm = jnp.where(causal, gw_ref[h], 0.0).astype(BF16)
                mixed = _bdot(wm, vb[:, cs]) + gb_ref[:, h:h + 1]
                p_scr[rs, cs] = (u_ref[rs, cs] * mixed).astype(BF16)

    @pl.when(jnp.logical_and(j == 0, i >= np_tiles))
    def _sample():
        for t in range(steps):
            rs = slice(t * slab, (t + 1) * slab)
            cv_ref[rs, :] = _rms(v_ref[rs, :], vg_ref[...])
        for t in range(steps):
            rs = slice(t * slab, (t + 1) * slab)
            for h in range(groups):
                cs = slice(h * gd, (h + 1) * gd)
                mixed = sw_ref[t * steps:t * steps + 1, cs] * cv_ref[0:slab, cs]
                for s in range(1, t + 1):
                    k = t * steps + s
                    mixed = mixed + sw_ref[k:k + 1, cs] * cv_ref[s * slab:(s + 1) * slab, cs]
                mixed = mixed + sb_ref[t:t + 1, cs]
                p_scr[rs, cs] = (u_ref[rs, cs] * mixed).astype(BF16)

    o_ref[...] = x_ref[...] + _bdot(p_scr[...], w_ref[0].astype(BF16))


def _chunk_out(z, x, v_gain, w_s, b_s, w_out, layer, *, m_p, slab, steps):
    m, d = x.shape
    a = z.shape[1] // 2
    groups, chunk = w_s.shape[0], w_s.shape[1]
    gd = a // groups
    assert m_p % TM == 0 and TM % chunk == 0 and m - m_p == TM == slab * steps
    assert steps <= chunk and slab % SUBLANES == 0 and a % TN == 0 and d % TN == 0
    np_tiles = m_p // TM
    na = a // TN
    gb = b_s.T
    sw = jnp.repeat(w_s[:, :steps, :steps].transpose(1, 2, 0).reshape(steps * steps, groups), gd, axis=1)
    sb = jnp.repeat(b_s[:, :steps].T, gd, axis=1)
    vm = (4 * TM * a * 4 + TM * a * 2 + 2 * groups * chunk * chunk * 4 + 2 * d * TN * 4 + d * TN * 2
          + 4 * TM * TN * 4 + 2 * TM * a * 4 + 4 * (steps * steps + steps) * a * 4)
    kern = functools.partial(_chunk_out_kernel, np_tiles=np_tiles, chunk=chunk, groups=groups,
                             slab=slab, steps=steps)
    return pl.pallas_call(
        kern,
        grid=(m // TM, d // TN),
        in_specs=[
            pl.BlockSpec((TM, a), lambda i, j: (i, 0)),
            pl.BlockSpec((TM, a), lambda i, j: (i, 1)),
            pl.BlockSpec((1, a), lambda i, j: (0, 0)),
            pl.BlockSpec((groups, chunk, chunk), lambda i, j: (0, 0, 0)),
            pl.BlockSpec((chunk, groups), lambda i, j: (0, 0)),
            pl.BlockSpec((steps * steps, a), lambda i, j: (0, 0)),
            pl.BlockSpec((steps, a), lambda i, j: (0, 0)),
            pl.BlockSpec((1, a, TN), lambda i, j: (layer, 0, j)),
            pl.BlockSpec((TM, TN), lambda i, j: (i, j)),
        ],
        out_specs=[pl.BlockSpec((TM, TN), lambda i, j: (i, j)),
                   pl.BlockSpec((TM, a), lambda i, j: (0, 0))],
        out_shape=[jax.ShapeDtypeStruct((m, d), F32), jax.ShapeDtypeStruct((TM, a), F32)],
        scratch_shapes=[pltpu.VMEM((TM, a), BF16)],
        compiler_params=pltpu.CompilerParams(
            dimension_semantics=("arbitrary", "arbitrary"),
            vmem_limit_bytes=_vmem_limit(vm)),
        name="chunk_out",
    )(z, z, v_gain.reshape(1, a), w_s, gb, sw, sb, w_out, x)


def _conv_out_kernel(bg_ref, z_ref, halo_ref, st_ref, cw_ref, w_ref, x_ref, o_ref, p_scr,
                     *, np_tiles, tiles_per_seq, slab, steps):
    i = pl.program_id(0)
    j = pl.program_id(1)
    width = cw_ref.shape[0]
    nhalo = halo_ref.shape[0]
    tm, cdim = p_scr.shape

    @pl.when(jnp.logical_and(j == 0, i < np_tiles))
    def _prompt():
        keep = (i % tiles_per_seq != 0).astype(F32)
        for c in range(cdim // TN):
            cs = slice(c * TN, (c + 1) * TN)
            ext = jnp.concatenate([halo_ref[:, cs] * keep, z_ref[:, cs]], axis=0)
            conv = cw_ref[0:1, cs] * pltpu.roll(ext, width - 1, axis=0)[nhalo:]
            for k in range(1, width):
                sh = width - 1 - k
                zk = pltpu.roll(ext, sh, axis=0)[nhalo:] if sh else z_ref[:, cs]
                conv = conv + cw_ref[k:k + 1, cs] * zk
            p_scr[:, cs] = (bg_ref[:, cs] * conv).astype(BF16)

    @pl.when(jnp.logical_and(j == 0, i >= np_tiles))
    def _sample():
        hist = width - 1

        def zrow(r, cs):
            if r < hist:
                return st_ref[r * slab:(r + 1) * slab, cs]
            return z_ref[(r - hist) * slab:(r - hist + 1) * slab, cs]

        for c in range(cdim // TN):
            cs = slice(c * TN, (c + 1) * TN)
            for t in range(steps):
                conv = cw_ref[0:1, cs] * zrow(t, cs)
                for k in range(1, width):
                    conv = conv + cw_ref[k:k + 1, cs] * zrow(t + k, cs)
                rs = slice(t * slab, (t + 1) * slab)
                p_scr[rs, cs] = (bg_ref[rs, cs] * conv).astype(BF16)

    o_ref[...] = x_ref[...] + _bdot(p_scr[...], w_ref[0].astype(BF16))


def _conv_out(bg, zc, x, state_t, conv_w, w_out, layer, *, m_p, seq, slab, steps):
    m, d = x.shape
    cd = zc.shape[1]
    width = conv_w.shape[0]
    assert m - m_p == TM == slab * steps and seq % TM == 0 and width - 1 <= SUBLANES
    np_tiles = m_p // TM
    halo_blocks = TM // SUBLANES
    vm = (4 * TM * cd * 4 + 2 * SUBLANES * cd * 4 + 2 * (width - 1) * slab * cd * 4 + TM * cd * 2
          + 2 * cd * TN * 4 + cd * TN * 2 + 4 * TM * TN * 4 + 6 * TM * TN * 4)
    kern = functools.partial(_conv_out_kernel, np_tiles=np_tiles, tiles_per_seq=seq // TM,
                             slab=slab, steps=steps)
    return pl.pallas_call(
        kern,
        grid=(m // TM, d // TN),
        in_specs=[
            pl.BlockSpec((TM, cd), lambda i, j: (i, 0)),
            pl.BlockSpec((TM, cd), lambda i, j: (i, 0)),
            pl.BlockSpec((SUBLANES, cd), lambda i, j: (jnp.maximum(i * halo_blocks - 1, 0), 0)),
            pl.BlockSpec(((width - 1) * slab, cd), lambda i, j: (0, 0)),
            pl.BlockSpec((width, cd), lambda i, j: (0, 0)),
            pl.BlockSpec((1, cd, TN), lambda i, j: (layer, 0, j)),
            pl.BlockSpec((TM, TN), lambda i, j: (i, j)),
        ],
        out_specs=pl.BlockSpec((TM, TN), lambda i, j: (i, j)),
        out_shape=jax.ShapeDtypeStruct((m, d), F32),
        scratch_shapes=[pltpu.VMEM((TM, cd), BF16)],
        compiler_params=pltpu.CompilerParams(
            dimension_semantics=("arbitrary", "arbitrary"),
            vmem_limit_bytes=_vmem_limit(vm)),
        name="conv_out",
    )(bg, zc, zc, state_t, conv_w, w_out, x)


def _pool_kernel(xf_ref, hf_ref, xg_ref, hg_ref, buf_ref, gf_ref, gg_ref, w_ref, sc_ref,
                 o_ref, hn_ref, r_scr, *, np_tiles, tiles_per_seq, slab, steps, windows, pos0):
    i = pl.program_id(0)
    g = pl.program_id(1)
    tm = xg_ref.shape[0]
    nhalo = hg_ref.shape[0]
    nbuf = buf_ref.shape[0] // slab

    @pl.when(g == 0)
    def _():
        xf = xf_ref[...]
        r_scr[nhalo:, :] = lax.rsqrt(jnp.mean(xf * xf, axis=-1, keepdims=True) + EPS)
        hf = hf_ref[...]
        r_scr[:nhalo, :] = lax.rsqrt(jnp.mean(hf * hf, axis=-1, keepdims=True) + EPS)

    h = xg_ref[...] * r_scr[nhalo:, :] * gg_ref[...]
    hn_ref[...] = h

    def finish(pooled):
        diff = (pooled - h).astype(BF16)
        y = _bdot(diff, w_ref[0].astype(BF16)) * sc_ref[...]
        o_ref[...] = xg_ref[...] + y

    for gi, win in enumerate(windows):
        @pl.when(jnp.logical_and(g == gi, i < np_tiles))
        def _prompt(win=win):
            seq_tile = i % tiles_per_seq
            keep = (seq_tile != 0).astype(F32)
            hh = hg_ref[...] * r_scr[:nhalo, :] * gg_ref[...] * keep
            s = jnp.concatenate([hh, h], axis=0)
            k = 1
            while k < win:
                s = s + pltpu.roll(s, k, axis=0)
                k *= 2
            pos = seq_tile * tm + lax.broadcasted_iota(jnp.int32, (tm, 1), 0)
            count = jnp.minimum(win, pos + 1).astype(F32)
            finish(s[nhalo:] / count)

        @pl.when(jnp.logical_and(g == gi, i >= np_tiles))
        def _sample(win=win):
            def hrow(r):
                if r < nbuf:
                    return buf_ref[r * slab:(r + 1) * slab, :]
                return h[(r - nbuf) * slab:(r - nbuf + 1) * slab]

            rows = []
            for t in range(steps):
                acc = hrow(nbuf + t - win + 1)
                for r in range(nbuf + t - win + 2, nbuf + t + 1):
                    acc = acc + hrow(r)
                rows.append(acc / float(min(win, pos0 + t + 1)))
            finish(jnp.concatenate(rows, axis=0))


def _pool_mixer(x, gain, buf_t, w_group, scale, *, m_p, seq, slab, steps):
    m, d = x.shape
    ng, gdim = w_group.shape[0], w_group.shape[1]
    nhalo = 2 * SUBLANES
    assert ng == len(POOL_WINDOWS) and max(POOL_WINDOWS) <= nhalo and gdim * ng == d
    assert m - m_p == TM == slab * steps and seq % TM == 0
    assert buf_t.shape[0] // slab >= max(POOL_WINDOWS) - 1
    np_tiles = m_p // TM
    halo_blocks = TM // nhalo
    nbuf_rows = buf_t.shape[0]
    vm = (2 * TM * d * 4 + 2 * nhalo * d * 4 + 2 * TM * gdim * 4 + 2 * nhalo * gdim * 4
          + 2 * nbuf_rows * gdim * 4 + 2 * gdim * gdim * 4 + gdim * gdim * 2
          + 4 * TM * gdim * 4 + (TM + nhalo) * LANES * 4 + 10 * TM * gdim * 4)
    kern = functools.partial(_pool_kernel, np_tiles=np_tiles, tiles_per_seq=seq // TM, slab=slab,
                             steps=steps, windows=POOL_WINDOWS, pos0=PAST_LEN)
    halo_idx = lambda i: jnp.maximum(i * halo_blocks - 1, 0)
    return pl.pallas_call(
        kern,
        grid=(m // TM, ng),
        in_specs=[
            pl.BlockSpec((TM, d), lambda i, g: (i, 0)),
            pl.BlockSpec((nhalo, d), lambda i, g: (halo_idx(i), 0)),
            pl.BlockSpec((TM, gdim), lambda i, g: (i, g)),
            pl.BlockSpec((nhalo, gdim), lambda i, g: (halo_idx(i), g)),
            pl.BlockSpec((nbuf_rows, gdim), lambda i, g: (0, g)),
            pl.BlockSpec((1, d), lambda i, g: (0, 0)),
            pl.BlockSpec((1, gdim), lambda i, g: (0, g)),
            pl.BlockSpec((1, gdim, gdim), lambda i, g: (g, 0, 0)),
            pl.BlockSpec((1, gdim), lambda i, g: (0, g)),
        ],
        out_specs=[pl.BlockSpec((TM, gdim), lambda i, g: (i, g)),
                   pl.BlockSpec((TM, gdim), lambda i, g: (i, g))],
        out_shape=[jax.ShapeDtypeStruct((m, d), F32), jax.ShapeDtypeStruct((m, d), F32)],
        scratch_shapes=[pltpu.VMEM((TM + nhalo, 1), F32)],
        compiler_params=pltpu.CompilerParams(
            dimension_semantics=("arbitrary", "arbitrary"),
            vmem_limit_bytes=_vmem_limit(vm)),
        name="pool_mixer",
    )(x, x, x, x, buf_t, gain.reshape(1, d), gain.reshape(1, d), w_group, scale.reshape(1, d))


def _ffn_kernel(ce_ref, ns_ref, nv_ref, x_ref, gain_ref, wg_ref, wu_ref, wd_ref, o_ref,
                xb_scr, *, dense):
    c = pl.program_id(0)
    f = pl.program_id(1)
    rows = x_ref.shape[0]
    nvalid = nv_ref[c]

    @pl.when(f == 0)
    def _init():
        for s in range(rows // FFN_SUB):
            rs = slice(s * FFN_SUB, (s + 1) * FFN_SUB)
            x = x_ref[rs, :]
            if dense:
                live = (s * FFN_SUB + lax.broadcasted_iota(jnp.int32, (FFN_SUB, 1), 0)) < nvalid
                o_ref[rs, :] = x
                x = jnp.where(live, _rms(x, gain_ref[...]), 0.0)
            else:
                o_ref[rs, :] = jnp.zeros_like(x)
            xb_scr[rs, :] = x.astype(BF16)

    def sub_tile(start, nrows):
        rs = pl.ds(start, nrows)
        xs = xb_scr[rs, :]
        gate = _bdot(xs, wg_ref[0].astype(BF16))
        up = _bdot(xs, wu_ref[0].astype(BF16))
        act = (gate * jax.nn.sigmoid(gate) * up).astype(BF16)
        o_ref[rs, :] += _bdot(act, wd_ref[0].astype(BF16))

    nsub = ns_ref[c]
    full = rows // FFN_SUB

    @pl.when(nsub == full)
    def _full():
        sub_tile(0, rows)

    @pl.when(nsub < full)
    def _partial():
        npair = lax.shift_right_logical(nsub, 1)

        def pair(s, carry):
            sub_tile(pl.multiple_of(s * (2 * FFN_SUB), 2 * FFN_SUB), 2 * FFN_SUB)
            return carry

        lax.fori_loop(0, npair, pair, 0)

        @pl.when(lax.bitwise_and(nsub, 1) == 1)
        def _tail():
            sub_tile(pl.multiple_of(npair * (2 * FFN_SUB), 2 * FFN_SUB), FFN_SUB)


def _ffn(x, gain, wg, wu, wd, chunk_expert, chunk_nsub, chunk_nvalid, *, dense):
    m, d = x.shape
    dff = wg.shape[-1]
    nch = chunk_expert.shape[0]
    nf = dff // FFN_TF
    assert dff % FFN_TF == 0 and FFN_ROWS % FFN_SUB == 0

    def w_in_map(c, f, ce, ns, nv):
        return (ce[c], 0, jnp.where(ns[c] > 0, f, nf - 1))

    def w_out_map(c, f, ce, ns, nv):
        return (ce[c], jnp.where(ns[c] > 0, f, nf - 1), 0)

    vm = (4 * FFN_ROWS * d * 4 + FFN_ROWS * d * 2 + 3 * (2 * d * FFN_TF * 4 + d * FFN_TF * 2)
          + 8 * FFN_SUB * FFN_TF * 4)
    grid_spec = pltpu.PrefetchScalarGridSpec(
        num_scalar_prefetch=3,
        grid=(nch, nf),
        in_specs=[
            pl.BlockSpec((FFN_ROWS, d), lambda c, f, ce, ns, nv: (c, 0)),
            pl.BlockSpec((1, d), lambda c, f, ce, ns, nv: (0, 0)),
            pl.BlockSpec((1, d, FFN_TF), w_in_map),
            pl.BlockSpec((1, d, FFN_TF), w_in_map),
            pl.BlockSpec((1, FFN_TF, d), w_out_map),
        ],
        out_specs=pl.BlockSpec((FFN_ROWS, d), lambda c, f, ce, ns, nv: (c, 0)),
        scratch_shapes=[pltpu.VMEM((FFN_ROWS, d), BF16)],
    )
    return pl.pallas_call(
        functools.partial(_ffn_kernel, dense=dense),
        grid_spec=grid_spec,
        out_shape=jax.ShapeDtypeStruct((m, d), F32),
        compiler_params=pltpu.CompilerParams(
            dimension_semantics=("arbitrary", "arbitrary"),
            vmem_limit_bytes=_vmem_limit(vm)),
        name="ffn_dense" if dense else "ffn_experts",
    )(chunk_expert, chunk_nsub, chunk_nvalid, x, gain.reshape(1, d), wg, wu, wd)


def _dense_ffn(x, gain, wg, wu, wd, layer):
    m = x.shape[0]
    nch = pl.cdiv(m, FFN_ROWS)
    nvalid = jnp.minimum(FFN_ROWS, m - FFN_ROWS * jnp.arange(nch, dtype=jnp.int32))
    nsub = (nvalid + FFN_SUB - 1) // FFN_SUB
    which = jnp.full((nch,), layer, jnp.int32)
    return _ffn(x, gain, wg, wu, wd, which, nsub, nvalid, dense=True)


def _route_kernel(x_ref, g_ref, r_ref, hn_ref, info_ref, cnt_ref, carry_scr, *, n_experts):
    i = pl.program_id(0)
    tm = x_ref.shape[0]

    @pl.when(i == 0)
    def _():
        carry_scr[...] = jnp.zeros_like(carry_scr)

    hn = _rms(x_ref[...], g_ref[...])
    hn_ref[...] = hn
    lane = lax.broadcasted_iota(jnp.int32, (tm, LANES), 1).astype(F32)
    logits = jnp.dot(hn, r_ref[...], preferred_element_type=F32, precision=lax.Precision.HIGHEST)
    logits = jnp.where(lane < n_experts, logits, -jnp.inf)
    m1 = jnp.max(logits, axis=-1, keepdims=True)
    i1 = jnp.min(jnp.where(logits == m1, lane, float(LANES)), axis=-1, keepdims=True)
    rest = jnp.where(lane == i1, -jnp.inf, logits)
    m2 = jnp.max(rest, axis=-1, keepdims=True)
    i2 = jnp.min(jnp.where(rest == m2, lane, float(LANES)), axis=-1, keepdims=True)
    e = jnp.exp(m2 - m1)
    g1 = 1.0 / (1.0 + e)
    g2 = e / (1.0 + e)
    oh1 = (lane == i1).astype(F32)
    oh2 = (lane == i2).astype(F32)
    cnt = oh1 + oh2
    row = lax.broadcasted_iota(jnp.int32, (tm, tm), 0)
    col = lax.broadcasted_iota(jnp.int32, (tm, tm), 1)
    before = (col < row).astype(BF16)
    ranks = _bdot(before, cnt.astype(BF16)) + carry_scr[0:1, :]
    rank1 = jnp.sum(ranks * oh1, axis=-1, keepdims=True)
    rank2 = jnp.sum(ranks * oh2, axis=-1, keepdims=True)
    carry_scr[0:1, :] = carry_scr[0:1, :] + jnp.sum(cnt, axis=0, keepdims=True)
    cnt_ref[...] = jnp.broadcast_to(carry_scr[0:1, :], cnt_ref.shape)
    info = jnp.where(lane == 0, i1, 0.0)
    info = jnp.where(lane == 1, i2, info)
    info = jnp.where(lane == 2, g1, info)
    info = jnp.where(lane == 3, g2, info)
    info = jnp.where(lane == 4, rank1, info)
    info = jnp.where(lane == 5, rank2, info)
    info_ref[...] = info


def _route(x, gain, router):
    m, d = x.shape
    ne = router.shape[1]
    rpad = jnp.pad(router, ((0, 0), (0, LANES - ne)))
    vm = 4 * TM * d * 4 + 2 * d * LANES * 4 + 4 * TM * LANES * 4 + TM * TM * 8 + 8 * TM * d * 4
    return pl.pallas_call(
        functools.partial(_route_kernel, n_experts=ne),
        grid=(m // TM,),
        in_specs=[
            pl.BlockSpec((TM, d), lambda i: (i, 0)),
            pl.BlockSpec((1, d), lambda i: (0, 0)),
            pl.BlockSpec((d, LANES), lambda i: (0, 0)),
        ],
        out_specs=[pl.BlockSpec((TM, d), lambda i: (i, 0)),
                   pl.BlockSpec((TM, LANES), lambda i: (i, 0)),
                   pl.BlockSpec((SUBLANES, LANES), lambda i: (0, 0))],
        out_shape=[jax.ShapeDtypeStruct((m, d), F32), jax.ShapeDtypeStruct((m, LANES), F32),
                   jax.ShapeDtypeStruct((SUBLANES, LANES), F32)],
        scratch_shapes=[pltpu.VMEM((SUBLANES, LANES), F32)],
        compiler_params=pltpu.CompilerParams(
            dimension_semantics=("arbitrary",), vmem_limit_bytes=_vmem_limit(vm)),
        name="route",
    )(x, gain.reshape(1, d), rpad)


def _row_copy(src, src_row, dst, dst_row, sem):
    return pltpu.make_async_copy(src.at[pl.ds(src_row, 1), :], dst.at[pl.ds(dst_row, 1), :], sem)


def _dispatch_kernel(p1_ref, p2_ref, hn_ref, init_ref, xs_ref, sem):
    del init_ref
    base = pl.program_id(0) * hn_ref.shape[0]
    tm = hn_ref.shape[0]

    def start(r, carry):
        _row_copy(hn_ref, r, xs_ref, p1_ref[base + r], sem).start()
        _row_copy(hn_ref, r, xs_ref, p2_ref[base + r], sem).start()
        return carry

    def wait(r, carry):
        _row_copy(hn_ref, r, xs_ref, p1_ref[base + r], sem).wait()
        _row_copy(hn_ref, r, xs_ref, p2_ref[base + r], sem).wait()
        return carry

    lax.fori_loop(0, tm, start, 0)
    lax.fori_loop(0, tm, wait, 0)


def _dispatch(hn, pos1, pos2, n_rows):
    m, d = hn.shape
    grid_spec = pltpu.PrefetchScalarGridSpec(
        num_scalar_prefetch=2,
        grid=(m // ROUTE_TM,),
        in_specs=[pl.BlockSpec((ROUTE_TM, d), lambda i, p1, p2: (i, 0)),
                  pl.BlockSpec(memory_space=pl.ANY)],
        out_specs=pl.BlockSpec(memory_space=pl.ANY),
        scratch_shapes=[pltpu.SemaphoreType.DMA(())],
    )
    return pl.pallas_call(
        _dispatch_kernel,
        grid_spec=grid_spec,
        out_shape=jax.ShapeDtypeStruct((n_rows, d), F32),
        input_output_aliases={3: 0},
        compiler_params=pltpu.CompilerParams(dimension_semantics=("arbitrary",)),
        name="dispatch",
    )(pos1, pos2, hn, jnp.zeros((n_rows, d), F32))


def _combine_kernel(p1_ref, p2_ref, x_ref, info_ref, gain_ref, ys_ref, o_ref, b1_scr, b2_scr, sem,
                    *, final_norm):
    tm = x_ref.shape[0]
    base = pl.program_id(0) * tm

    def start(r, carry):
        _row_copy(ys_ref, p1_ref[base + r], b1_scr, r, sem).start()
        _row_copy(ys_ref, p2_ref[base + r], b2_scr, r, sem).start()
        return carry

    def wait(r, carry):
        _row_copy(ys_ref, p1_ref[base + r], b1_scr, r, sem).wait()
        _row_copy(ys_ref, p2_ref[base + r], b2_scr, r, sem).wait()
        return carry

    lax.fori_loop(0, tm, start, 0)
    lax.fori_loop(0, tm, wait, 0)
    info = info_ref[...]
    e1, e2 = info[:, 0:1], info[:, 1:2]
    g1, g2 = info[:, 2:3], info[:, 3:4]
    y1 = g1 * b1_scr[...]
    y2 = g2 * b2_scr[...]
    lo = jnp.where(e1 < e2, y1, y2)
    hi = jnp.where(e1 < e2, y2, y1)
    out = x_ref[...] + (lo + hi)
    if final_norm:
        out = _rms(out, gain_ref[...])
    o_ref[...] = out


def _combine(x, info, ys, pos1, pos2, gain, *, final_norm):
    m, d = x.shape
    vm = 4 * ROUTE_TM * d * 4 + 2 * ROUTE_TM * LANES * 4 + 2 * ROUTE_TM * d * 4 + 6 * ROUTE_TM * d * 4
    grid_spec = pltpu.PrefetchScalarGridSpec(
        num_scalar_prefetch=2,
        grid=(m // ROUTE_TM,),
        in_specs=[
            pl.BlockSpec((ROUTE_TM, d), lambda i, p1, p2: (i, 0)),
            pl.BlockSpec((ROUTE_TM, LANES), lambda i, p1, p2: (i, 0)),
            pl.BlockSpec((1, d), lambda i, p1, p2: (0, 0)),
            pl.BlockSpec(memory_space=pl.ANY),
        ],
        out_specs=pl.BlockSpec((ROUTE_TM, d), lambda i, p1, p2: (i, 0)),
        scratch_shapes=[pltpu.VMEM((ROUTE_TM, d), F32), pltpu.VMEM((ROUTE_TM, d), F32),
                        pltpu.SemaphoreType.DMA(())],
    )
    return pl.pallas_call(
        functools.partial(_combine_kernel, final_norm=final_norm),
        grid_spec=grid_spec,
        out_shape=jax.ShapeDtypeStruct((m, d), F32),
        compiler_params=pltpu.CompilerParams(
            dimension_semantics=("arbitrary",), vmem_limit_bytes=_vmem_limit(vm)),
        name="combine_final" if final_norm else "combine",
    )(pos1, pos2, x, info, gain.reshape(1, d), ys)


def _moe_ffn(x, gain, router, wg, wu, wd, layer, out_gain, *, final_norm):
    m, d = x.shape
    ne = router.shape[1]
    wg, wu, wd = (w.reshape((-1,) + w.shape[2:]) for w in (wg, wu, wd))
    assert m % TM == 0 and m % ROUTE_TM == 0
    hn, info, cnt = _route(x, gain, router)
    counts = cnt[0, :ne].astype(jnp.int32)
    chunks_per = (counts + FFN_ROWS - 1) // FFN_ROWS
    chunk_end = jnp.cumsum(chunks_per)
    offsets = (chunk_end - chunks_per) * FFN_ROWS
    e1 = info[:, 0].astype(jnp.int32)
    e2 = info[:, 1].astype(jnp.int32)
    pos1 = offsets[e1] + info[:, 4].astype(jnp.int32)
    pos2 = offsets[e2] + info[:, 5].astype(jnp.int32)
    nch = (m * TOP_K) // FFN_ROWS + ne
    cidx = jnp.arange(nch, dtype=jnp.int32)
    last = jnp.maximum(chunk_end[-1] - 1, 0)
    owner = jnp.sum(jnp.minimum(cidx, last)[:, None] >= chunk_end[None, :], axis=1).astype(jnp.int32)
    owner = jnp.minimum(owner, ne - 1)
    local = cidx - (chunk_end - chunks_per)[owner]
    nvalid = jnp.clip(counts[owner] - local * FFN_ROWS, 0, FFN_ROWS)
    nvalid = jnp.where(cidx < chunk_end[-1], nvalid, 0).astype(jnp.int32)
    nsub = (nvalid + FFN_SUB - 1) // FFN_SUB
    xs = _dispatch(hn, pos1, pos2, nch * FFN_ROWS)
    ys = _ffn(xs, gain, wg, wu, wd, owner + layer * ne, nsub, nvalid, dense=False)
    return _combine(x, info, ys, pos1, pos2, out_gain, final_norm=final_norm)


def kernel(x_prompt, x_sample, state_conv, state_pool, norm_mix, norm_ffn, final_norm, a_w_in, a_v_gain, a_w_s, a_b_s, a_w_out, b_w_in, b_conv, b_w_out, c_w_group, c_scale, ffn_w_gate, ffn_w_up, ffn_w_down, moe_router, moe_w_gate, moe_w_up, moe_w_down):
    batch, seq, d = x_prompt.shape
    slab, steps, _ = x_sample.shape
    depth = norm_mix.shape[0]
    assert depth % 2 == 0, "the final RMSNorm is fused into the routed combine kernel of the last layer"
    m_p = batch * seq
    geo = dict(m_p=m_p, slab=slab, steps=steps)

    def to_time_major(a):
        return a.transpose(1, 0, 2).reshape(a.shape[1] * slab, a.shape[2])

    def from_time_major(a, r):
        return a.reshape(r, slab, a.shape[-1]).transpose(1, 0, 2)

    def seq_tails(a, r):
        return jnp.stack([a[(b + 1) * seq - r:(b + 1) * seq] for b in range(batch)])

    x = jnp.concatenate([x_prompt.reshape(m_p, d), to_time_major(x_sample)], axis=0)
    chunk_v_s, conv_p, conv_s, pool_p, pool_s = [], [], [], [], []
    for i in range(depth):
        j, kind = divmod(i, 3)
        if kind == 0:
            z = _inproj_gelu(x, norm_mix[i], a_w_in, j)
            x, cv = _chunk_out(z, x, a_v_gain[j], a_w_s[j], a_b_s[j], a_w_out, j, **geo)
            chunk_v_s.append(from_time_major(cv, steps))
        elif kind == 1:
            hist = b_conv.shape[1] - 1
            bg, zc = _inproj_conv(x, norm_mix[i], b_w_in, j)
            x = _conv_out(bg, zc, x, to_time_major(state_conv[j]), b_conv[j], b_w_out, j, seq=seq, **geo)
            conv_p.append(seq_tails(zc, hist))
            conv_s.append(from_time_major(zc[m_p:], steps)[:, steps - hist:])
        else:
            nbuf = state_pool.shape[2]
            x, hn = _pool_mixer(x, norm_mix[i], to_time_major(state_pool[j]), c_w_group[j], c_scale[j],
                                seq=seq, **geo)
            pool_p.append(seq_tails(hn, nbuf))
            hc = jnp.concatenate([state_pool[j], from_time_major(hn[m_p:], steps)], axis=1)
            pool_s.append(hc[:, hc.shape[1] - nbuf:])
        f = i // 2
        if i % 2 == 0:
            x = _dense_ffn(x, norm_ffn[i], ffn_w_gate, ffn_w_up, ffn_w_down, f)
        else:
            x = _moe_ffn(x, norm_ffn[i], moe_router[f], moe_w_gate, moe_w_up, moe_w_down, f,
                         final_norm, final_norm=(i == depth - 1))
    y_prompt = x[:m_p].reshape(batch, seq, d)
    y_sample = from_time_major(x[m_p:], steps)
    return (y_prompt, y_sample, jnp.stack(chunk_v_s), jnp.stack(conv_p), jnp.stack(conv_s),
            jnp.stack(pool_p), jnp.stack(pool_s))
```

```python
import functools
import math

import jax
import jax.numpy as jnp
from jax import lax
from jax.experimental import pallas as pl
from jax.experimental.pallas import tpu as pltpu

F32 = jnp.float32
BF16 = jnp.bfloat16
EPS = 1e-6
PAST_LEN = 16384
POOL_WINDOWS = (2, 4, 8, 16)
TOP_K = 2

V7X_VMEM_BYTES = 64 * 1024 * 1024
SUBLANES = 8
LANES = 128

TM = 512
TN = 512
FFN_UNIT = 128
FFN_ROWS = 9 * FFN_UNIT
FFN_TF = 512
ROUTE_TM = 256
ROW_DMA_UNROLL = 8


def _vmem_limit(nbytes):
    return min(int(nbytes) + (6 << 20), V7X_VMEM_BYTES - (4 << 20))


def _rms(x, gain):
    ms = jnp.mean(x * x, axis=-1, keepdims=True)
    return x * lax.rsqrt(ms + EPS) * gain


def _gelu_tanh(x):
    c = math.sqrt(2.0 / math.pi)
    return 0.5 * x * (1.0 + jnp.tanh(c * (x + 0.044715 * (x * x * x))))


def _bdot(a, b):
    return jnp.dot(a, b, preferred_element_type=F32)


INPROJ_ROW_TILES = 8


def _inproj_rows(m):
    assert m % (INPROJ_ROW_TILES * 2 * SUBLANES) == 0
    return m // INPROJ_ROW_TILES


def _inproj_gelu_kernel(x_ref, g_ref, w_ref, o_ref, h_scr):
    @pl.when(pl.program_id(1) == 0)
    def _():
        h_scr[...] = _rms(x_ref[...], g_ref[...]).astype(BF16)

    z = _bdot(h_scr[...], w_ref[0].astype(BF16))
    o_ref[...] = _gelu_tanh(z)


def _inproj_gelu(x, gain, w, layer):
    m, d = x.shape
    n = w.shape[2]
    tm = _inproj_rows(m)
    vm = 2 * tm * d * 4 + tm * d * 2 + 2 * d * TN * 4 + d * TN * 2 + 4 * tm * TN * 4
    return pl.pallas_call(
        _inproj_gelu_kernel,
        grid=(m // tm, n // TN),
        in_specs=[
            pl.BlockSpec((tm, d), lambda i, j: (i, 0)),
            pl.BlockSpec((1, d), lambda i, j: (0, 0)),
            pl.BlockSpec((1, d, TN), lambda i, j: (layer, 0, j)),
        ],
        out_specs=pl.BlockSpec((tm, TN), lambda i, j: (i, j)),
        out_shape=jax.ShapeDtypeStruct((m, n), F32),
        scratch_shapes=[pltpu.VMEM((tm, d), BF16)],
        compiler_params=pltpu.CompilerParams(
            dimension_semantics=("arbitrary", "arbitrary"),
            vmem_limit_bytes=_vmem_limit(vm)),
        name="inproj_gelu",
    )(x, gain.reshape(1, d), w)


def _inproj_conv_kernel(x_ref, g_ref, wb_ref, wc_ref, wh_ref, ob_ref, oz_ref, h_scr):
    @pl.when(pl.program_id(1) == 0)
    def _():
        h_scr[...] = _rms(x_ref[...], g_ref[...]).astype(BF16)

    h = h_scr[...]
    ob_ref[...] = _bdot(h, wb_ref[0].astype(BF16))
    c = _bdot(h, wc_ref[0].astype(BF16))
    hin = _bdot(h, wh_ref[0].astype(BF16))
    oz_ref[...] = c * hin


def _inproj_conv(x, gain, w, layer):
    m, d = x.shape
    cd = w.shape[2] // 3
    tm = _inproj_rows(m)
    tn = TN // 2
    nb = cd // tn
    vm = 2 * tm * d * 4 + tm * d * 2 + 3 * (2 * d * tn * 4 + d * tn * 2) + 10 * tm * tn * 4
    out = jax.ShapeDtypeStruct((m, cd), F32)
    return pl.pallas_call(
        _inproj_conv_kernel,
        grid=(m // tm, nb),
        in_specs=[
            pl.BlockSpec((tm, d), lambda i, j: (i, 0)),
            pl.BlockSpec((1, d), lambda i, j: (0, 0)),
            pl.BlockSpec((1, d, tn), lambda i, j: (layer, 0, j)),
            pl.BlockSpec((1, d, tn), lambda i, j: (layer, 0, j + nb)),
            pl.BlockSpec((1, d, tn), lambda i, j: (layer, 0, j + 2 * nb)),
        ],
        out_specs=[pl.BlockSpec((tm, tn), lambda i, j: (i, j)),
                   pl.BlockSpec((tm, tn), lambda i, j: (i, j))],
        out_shape=[out, out],
        scratch_shapes=[pltpu.VMEM((tm, d), BF16)],
        compiler_params=pltpu.CompilerParams(
            dimension_semantics=("arbitrary", "arbitrary"),
            vmem_limit_bytes=_vmem_limit(vm)),
        name="inproj_conv",
    )(x, gain.reshape(1, d), w, w, w)


def _chunk_out_kernel(u_ref, v_ref, vg_ref, gw_ref, gb_ref, sw_ref, sb_ref, w_ref, x_ref,
                      o_ref, cv_ref, p_scr, *, np_tiles, chunk, groups, slab, steps):
    i = pl.program_id(0)
    gd = p_scr.shape[1] // groups

    @pl.when(i < np_tiles)
    def _prompt():
        row = lax.broadcasted_iota(jnp.int32, (chunk, chunk), 0)
        col = lax.broadcasted_iota(jnp.int32, (chunk, chunk), 1)
        causal = col <= row
        for s in range(p_scr.shape[0] // chunk):
            rs = slice(s * chunk, (s + 1) * chunk)
            vb = _rms(v_ref[rs, :], vg_ref[...]).astype(BF16)
            for h in range(groups):
                cs = slice(h * gd, (h + 1) * gd)
                wm = jnp.where(causal, gw_ref[h], 0.0).astype(BF16)
                mixed = _bdot(wm, vb[:, cs]) + gb_ref[:, h:h + 1]
                p_scr[rs, cs] = (u_ref[rs, cs] * mixed).astype(BF16)

    @pl.when(i >= np_tiles)
    def _sample():
        for t in range(steps):
            rs = slice(t * slab, (t + 1) * slab)
            cv_ref[rs, :] = _rms(v_ref[rs, :], vg_ref[...])
        for t in range(steps):
            rs = slice(t * slab, (t + 1) * slab)
            for h in range(groups):
                cs = slice(h * gd, (h + 1) * gd)
                mixed = sw_ref[t * steps:t * steps + 1, cs] * cv_ref[0:slab, cs]
                for s in range(1, t + 1):
                    k = t * steps + s
                    mixed = mixed + sw_ref[k:k + 1, cs] * cv_ref[s * slab:(s + 1) * slab, cs]
                mixed = mixed + sb_ref[t:t + 1, cs]
                p_scr[rs, cs] = (u_ref[rs, cs] * mixed).astype(BF16)

    o_ref[...] = x_ref[...] + _bdot(p_scr[...], w_ref[0])


def _chunk_out(z, x, v_gain, w_s, b_s, w_out, layer, *, m_p, slab, steps):
    m, d = x.shape
    a = z.shape[1] // 2
    groups, chunk = w_s.shape[0], w_s.shape[1]
    gd = a // groups
    assert m_p % TM == 0 and TM % chunk == 0 and m - m_p == TM == slab * steps
    assert steps <= chunk and slab % SUBLANES == 0 and w_out.dtype == BF16
    np_tiles = m_p // TM
    gb = b_s.T
    sw = jnp.repeat(w_s[:, :steps, :steps].transpose(1, 2, 0).reshape(steps * steps, groups), gd, axis=1)
    sb = jnp.repeat(b_s[:, :steps].T, gd, axis=1)
    vm = (4 * TM * a * 4 + TM * a * 2 + 2 * groups * chunk * chunk * 4 + a * d * 2
          + 4 * TM * d * 4 + TM * a * 4 + 4 * (steps * steps + steps) * a * 4 + 2 * TM * d * 4)
    once = pl.Buffered(1)
    kern = functools.partial(_chunk_out_kernel, np_tiles=np_tiles, chunk=chunk, groups=groups,
                             slab=slab, steps=steps)
    return pl.pallas_call(
        kern,
        grid=(m // TM,),
        in_specs=[
            pl.BlockSpec((TM, a), lambda i: (i, 0)),
            pl.BlockSpec((TM, a), lambda i: (i, 1)),
            pl.BlockSpec((1, a), lambda i: (0, 0)),
            pl.BlockSpec((groups, chunk, chunk), lambda i: (0, 0, 0)),
            pl.BlockSpec((chunk, groups), lambda i: (0, 0)),
            pl.BlockSpec((steps * steps, a), lambda i: (0, 0)),
            pl.BlockSpec((steps, a), lambda i: (0, 0)),
            pl.BlockSpec((1, a, d), lambda i: (layer, 0, 0), pipeline_mode=once),
            pl.BlockSpec((TM, d), lambda i: (i, 0)),
        ],
        out_specs=[pl.BlockSpec((TM, d), lambda i: (i, 0)),
                   pl.BlockSpec((TM, a), lambda i: (0, 0), pipeline_mode=once)],
        out_shape=[jax.ShapeDtypeStruct((m, d), F32), jax.ShapeDtypeStruct((TM, a), F32)],
        scratch_shapes=[pltpu.VMEM((TM, a), BF16)],
        compiler_params=pltpu.CompilerParams(
            dimension_semantics=("arbitrary",),
            vmem_limit_bytes=_vmem_limit(vm)),
        name="chunk_out",
    )(z, z, v_gain.reshape(1, a), w_s, gb, sw, sb, w_out, x)


def _conv_out_kernel(bg_ref, z_ref, halo_ref, st_ref, cw_ref, w_ref, x_ref, o_ref, p_scr,
                     *, np_tiles, tiles_per_seq, slab, steps):
    i = pl.program_id(0)
    width = cw_ref.shape[0]
    nhalo = halo_ref.shape[0]
    tm, cdim = p_scr.shape

    @pl.when(i < np_tiles)
    def _prompt():
        keep = (i % tiles_per_seq != 0).astype(F32)
        for c in range(cdim // TN):
            cs = slice(c * TN, (c + 1) * TN)
            ext = jnp.concatenate([halo_ref[:, cs] * keep, z_ref[:, cs]], axis=0)
            conv = cw_ref[0:1, cs] * pltpu.roll(ext, width - 1, axis=0)[nhalo:]
            for k in range(1, width):
                sh = width - 1 - k
                zk = pltpu.roll(ext, sh, axis=0)[nhalo:] if sh else z_ref[:, cs]
                conv = conv + cw_ref[k:k + 1, cs] * zk
            p_scr[:, cs] = (bg_ref[:, cs] * conv).astype(BF16)

    @pl.when(i >= np_tiles)
    def _sample():
        hist = width - 1

        def zrow(r, cs):
            if r < hist:
                return st_ref[r * slab:(r + 1) * slab, cs]
            return z_ref[(r - hist) * slab:(r - hist + 1) * slab, cs]

        for c in range(cdim // TN):
            cs = slice(c * TN, (c + 1) * TN)
            for t in range(steps):
                conv = cw_ref[0:1, cs] * zrow(t, cs)
                for k in range(1, width):
                    conv = conv + cw_ref[k:k + 1, cs] * zrow(t + k, cs)
                rs = slice(t * slab, (t + 1) * slab)
                p_scr[rs, cs] = (bg_ref[rs, cs] * conv).astype(BF16)

    o_ref[...] = x_ref[...] + _bdot(p_scr[...], w_ref[0])


def _conv_out(bg, zc, x, state_t, conv_w, w_out, layer, *, m_p, seq, slab, steps):
    m, d = x.shape
    cd = zc.shape[1]
    width = conv_w.shape[0]
    assert m - m_p == TM == slab * steps and seq % TM == 0 and width - 1 <= SUBLANES
    assert w_out.dtype == BF16
    np_tiles = m_p // TM
    halo_blocks = TM // SUBLANES
    vm = (4 * TM * cd * 4 + 2 * SUBLANES * cd * 4 + (width - 1) * slab * cd * 4 + TM * cd * 2
          + cd * d * 2 + 4 * TM * d * 4 + 2 * TM * d * 4 + 6 * TM * TN * 4)
    kern = functools.partial(_conv_out_kernel, np_tiles=np_tiles, tiles_per_seq=seq // TM,
                             slab=slab, steps=steps)
    once = pl.Buffered(1)
    return pl.pallas_call(
        kern,
        grid=(m // TM,),
        in_specs=[
            pl.BlockSpec((TM, cd), lambda i: (i, 0)),
            pl.BlockSpec((TM, cd), lambda i: (i, 0)),
            pl.BlockSpec((SUBLANES, cd), lambda i: (jnp.maximum(i * halo_blocks - 1, 0), 0)),
            pl.BlockSpec(((width - 1) * slab, cd), lambda i: (0, 0), pipeline_mode=once),
            pl.BlockSpec((width, cd), lambda i: (0, 0)),
            pl.BlockSpec((1, cd, d), lambda i: (layer, 0, 0), pipeline_mode=once),
            pl.BlockSpec((TM, d), lambda i: (i, 0)),
        ],
        out_specs=pl.BlockSpec((TM, d), lambda i: (i, 0)),
        out_shape=jax.ShapeDtypeStruct((m, d), F32),
        scratch_shapes=[pltpu.VMEM((TM, cd), BF16)],
        compiler_params=pltpu.CompilerParams(
            dimension_semantics=("arbitrary",),
            vmem_limit_bytes=_vmem_limit(vm)),
        name="conv_out",
    )(bg, zc, zc, state_t, conv_w, w_out, x)


def _pool_kernel(xf_ref, hf_ref, xg_ref, hg_ref, buf_ref, gf_ref, gg_ref, w_ref, sc_ref,
                 o_ref, hn_ref, r_scr, *, np_tiles, tiles_per_seq, slab, steps, windows, pos0):
    i = pl.program_id(0)
    g = pl.program_id(1)
    tm = xg_ref.shape[0]
    nhalo = hg_ref.shape[0]
    nbuf = buf_ref.shape[0] // slab

    @pl.when(g == 0)
    def _():
        xf = xf_ref[...]
        r_scr[nhalo:, :] = lax.rsqrt(jnp.mean(xf * xf, axis=-1, keepdims=True) + EPS)
        hf = hf_ref[...]
        r_scr[:nhalo, :] = lax.rsqrt(jnp.mean(hf * hf, axis=-1, keepdims=True) + EPS)

    h = xg_ref[...] * r_scr[nhalo:, :] * gg_ref[...]
    hn_ref[...] = h

    def finish(pooled):
        diff = (pooled - h).astype(BF16)
        y = _bdot(diff, w_ref[0].astype(BF16)) * sc_ref[...]
        o_ref[...] = xg_ref[...] + y

    for gi, win in enumerate(windows):
        @pl.when(jnp.logical_and(g == gi, i < np_tiles))
        def _prompt(win=win):
            seq_tile = i % tiles_per_seq
            keep = (seq_tile != 0).astype(F32)
            hh = hg_ref[...] * r_scr[:nhalo, :] * gg_ref[...] * keep
            s = jnp.concatenate([hh, h], axis=0)
            k = 1
            while k < win:
                s = s + pltpu.roll(s, k, axis=0)
                k *= 2
            pos = seq_tile * tm + lax.broadcasted_iota(jnp.int32, (tm, 1), 0)
            count = jnp.minimum(win, pos + 1).astype(F32)
            finish(s[nhalo:] / count)

        @pl.when(jnp.logical_and(g == gi, i >= np_tiles))
        def _sample(win=win):
            def hrow(r):
                if r < nbuf:
                    return buf_ref[r * slab:(r + 1) * slab, :]
                return h[(r - nbuf) * slab:(r - nbuf + 1) * slab]

            rows = []
            for t in range(steps):
                acc = hrow(nbuf + t - win + 1)
                for r in range(nbuf + t - win + 2, nbuf + t + 1):
                    acc = acc + hrow(r)
                rows.append(acc / float(min(win, pos0 + t + 1)))
            finish(jnp.concatenate(rows, axis=0))


def _pool_mixer(x, gain, buf_t, w_group, scale, *, m_p, seq, slab, steps):
    m, d = x.shape
    ng, gdim = w_group.shape[0], w_group.shape[1]
    nhalo = 2 * SUBLANES
    assert ng == len(POOL_WINDOWS) and max(POOL_WINDOWS) <= nhalo and gdim * ng == d
    assert m - m_p == TM == slab * steps and seq % TM == 0
    assert buf_t.shape[0] // slab >= max(POOL_WINDOWS) - 1
    np_tiles = m_p // TM
    halo_blocks = TM // nhalo
    nbuf_rows = buf_t.shape[0]
    vm = (2 * TM * d * 4 + 2 * nhalo * d * 4 + 2 * TM * gdim * 4 + 2 * nhalo * gdim * 4
          + 2 * nbuf_rows * gdim * 4 + 2 * gdim * gdim * 4 + gdim * gdim * 2
          + 4 * TM * gdim * 4 + (TM + nhalo) * LANES * 4 + 10 * TM * gdim * 4)
    kern = functools.partial(_pool_kernel, np_tiles=np_tiles, tiles_per_seq=seq // TM, slab=slab,
                             steps=steps, windows=POOL_WINDOWS, pos0=PAST_LEN)
    halo_idx = lambda i: jnp.maximum(i * halo_blocks - 1, 0)
    return pl.pallas_call(
        kern,
        grid=(m // TM, ng),
        in_specs=[
            pl.BlockSpec((TM, d), lambda i, g: (i, 0)),
            pl.BlockSpec((nhalo, d), lambda i, g: (halo_idx(i), 0)),
            pl.BlockSpec((TM, gdim), lambda i, g: (i, g)),
            pl.BlockSpec((nhalo, gdim), lambda i, g: (halo_idx(i), g)),
            pl.BlockSpec((nbuf_rows, gdim), lambda i, g: (0, jnp.where(i >= np_tiles, g, 0))),
            pl.BlockSpec((1, d), lambda i, g: (0, 0)),
            pl.BlockSpec((1, gdim), lambda i, g: (0, g)),
            pl.BlockSpec((1, gdim, gdim), lambda i, g: (g, 0, 0)),
            pl.BlockSpec((1, gdim), lambda i, g: (0, g)),
        ],
        out_specs=[pl.BlockSpec((TM, gdim), lambda i, g: (i, g)),
                   pl.BlockSpec((TM, gdim), lambda i, g: (i, g))],
        out_shape=[jax.ShapeDtypeStruct((m, d), F32), jax.ShapeDtypeStruct((m, d), F32)],
        scratch_shapes=[pltpu.VMEM((TM + nhalo, 1), F32)],
        compiler_params=pltpu.CompilerParams(
            dimension_semantics=("arbitrary", "arbitrary"),
            vmem_limit_bytes=_vmem_limit(vm)),
        name="pool_mixer",
    )(x, x, x, x, buf_t, gain.reshape(1, d), gain.reshape(1, d), w_group, scale.reshape(1, d))


def _ffn_kernel(ce_ref, nu_ref, nv_ref, x_ref, gain_ref, wg_ref, wu_ref, wd_ref, o_ref,
                xb_scr, *, dense):
    c = pl.program_id(0)
    f = pl.program_id(1)
    rows = x_ref.shape[0]
    nvalid = nv_ref[c]

    @pl.when(f == 0)
    def _init():
        for s in range(rows // FFN_UNIT):
            rs = slice(s * FFN_UNIT, (s + 1) * FFN_UNIT)
            x = x_ref[rs, :]
            if dense:
                live = (s * FFN_UNIT + lax.broadcasted_iota(jnp.int32, (FFN_UNIT, 1), 0)) < nvalid
                o_ref[rs, :] = x
                x = jnp.where(live, _rms(x, gain_ref[...]), 0.0)
            else:
                o_ref[rs, :] = jnp.zeros_like(x)
            xb_scr[rs, :] = x.astype(BF16)

    def sub_tile(start, nrows):
        rs = pl.ds(pl.multiple_of(start, FFN_UNIT), nrows)
        xs = xb_scr[rs, :]
        gate = _bdot(xs, wg_ref[0].astype(BF16))
        up = _bdot(xs, wu_ref[0].astype(BF16))
        act = (gate * jax.nn.sigmoid(gate) * up).astype(BF16)
        o_ref[rs, :] += _bdot(act, wd_ref[0].astype(BF16))

    units = nu_ref[c]
    full = rows // FFN_UNIT

    @pl.when(units == full)
    def _full():
        sub_tile(0, rows)

    @pl.when(units < full)
    def _partial():
        nquad = lax.shift_right_logical(units, 2)

        def quad(s, carry):
            sub_tile(s * (4 * FFN_UNIT), 4 * FFN_UNIT)
            return carry

        lax.fori_loop(0, nquad, quad, 0)
        has_pair = lax.bitwise_and(units, 2)

        @pl.when(has_pair != 0)
        def _pair():
            sub_tile(nquad * (4 * FFN_UNIT), 2 * FFN_UNIT)

        @pl.when(lax.bitwise_and(units, 1) != 0)
        def _single():
            sub_tile(nquad * (4 * FFN_UNIT) + has_pair * FFN_UNIT, FFN_UNIT)


def _ffn(x, gain, wg, wu, wd, chunk_expert, chunk_units, chunk_nvalid, *, dense):
    m, d = x.shape
    dff = wg.shape[-1]
    nch = chunk_expert.shape[0]
    nf = dff // FFN_TF
    assert dff % FFN_TF == 0

    def w_in_map(c, f, ce, nu, nv):
        return (ce[c], 0, jnp.where(nu[c] > 0, f, nf - 1))

    def w_out_map(c, f, ce, nu, nv):
        return (ce[c], jnp.where(nu[c] > 0, f, nf - 1), 0)

    vm = (2 * FFN_ROWS * d * 4 + FFN_ROWS * d * 2 + 3 * 2 * d * FFN_TF * 4
          + 3 * d * FFN_TF * 2 + 16 * FFN_UNIT * FFN_TF * 4)
    once = pl.Buffered(1)
    grid_spec = pltpu.PrefetchScalarGridSpec(
        num_scalar_prefetch=3,
        grid=(nch, nf),
        in_specs=[
            pl.BlockSpec((FFN_ROWS, d), lambda c, f, ce, nu, nv: (c, 0), pipeline_mode=once),
            pl.BlockSpec((1, d), lambda c, f, ce, nu, nv: (0, 0)),
            pl.BlockSpec((1, d, FFN_TF), w_in_map),
            pl.BlockSpec((1, d, FFN_TF), w_in_map),
            pl.BlockSpec((1, FFN_TF, d), w_out_map),
        ],
        out_specs=pl.BlockSpec((FFN_ROWS, d), lambda c, f, ce, nu, nv: (c, 0), pipeline_mode=once),
        scratch_shapes=[pltpu.VMEM((FFN_ROWS, d), BF16)],
    )
    return pl.pallas_call(
        functools.partial(_ffn_kernel, dense=dense),
        grid_spec=grid_spec,
        out_shape=jax.ShapeDtypeStruct((m, d), F32),
        compiler_params=pltpu.CompilerParams(
            dimension_semantics=("arbitrary", "arbitrary"),
            vmem_limit_bytes=_vmem_limit(vm)),
        name="ffn_dense" if dense else "ffn_experts",
    )(chunk_expert, chunk_units, chunk_nvalid, x, gain.reshape(1, d), wg, wu, wd)


def _dense_ffn(x, gain, wg, wu, wd, layer):
    m = x.shape[0]
    nch = pl.cdiv(m, FFN_ROWS)
    nvalid = jnp.minimum(FFN_ROWS, m - FFN_ROWS * jnp.arange(nch, dtype=jnp.int32))
    units = (nvalid + FFN_UNIT - 1) // FFN_UNIT
    which = jnp.full((nch,), layer, jnp.int32)
    return _ffn(x, gain, wg, wu, wd, which, units, nvalid, dense=True)


def _route_kernel(x_ref, g_ref, r_ref, hn_ref, info_ref, cnt_ref, carry_scr, *, n_experts):
    i = pl.program_id(0)
    tm = x_ref.shape[0]

    @pl.when(i == 0)
    def _():
        carry_scr[...] = jnp.zeros_like(carry_scr)

    hn = _rms(x_ref[...], g_ref[...])
    hn_ref[...] = hn
    lane = lax.broadcasted_iota(jnp.int32, (tm, LANES), 1).astype(F32)
    logits = jnp.dot(hn, r_ref[...], preferred_element_type=F32, precision=lax.Precision.HIGHEST)
    logits = jnp.where(lane < n_experts, logits, -jnp.inf)
    m1 = jnp.max(logits, axis=-1, keepdims=True)
    i1 = jnp.min(jnp.where(logits == m1, lane, float(LANES)), axis=-1, keepdims=True)
    rest = jnp.where(lane == i1, -jnp.inf, logits)
    m2 = jnp.max(rest, axis=-1, keepdims=True)
    i2 = jnp.min(jnp.where(rest == m2, lane, float(LANES)), axis=-1, keepdims=True)
    e = jnp.exp(m2 - m1)
    g1 = 1.0 / (1.0 + e)
    g2 = e / (1.0 + e)
    oh1 = (lane == i1).astype(F32)
    oh2 = (lane == i2).astype(F32)
    cnt = oh1 + oh2
    row = lax.broadcasted_iota(jnp.int32, (tm, tm), 0)
    col = lax.broadcasted_iota(jnp.int32, (tm, tm), 1)
    before = (col < row).astype(BF16)
    ranks = _bdot(before, cnt.astype(BF16)) + carry_scr[0:1, :]
    rank1 = jnp.sum(ranks * oh1, axis=-1, keepdims=True)
    rank2 = jnp.sum(ranks * oh2, axis=-1, keepdims=True)
    carry_scr[0:1, :] = carry_scr[0:1, :] + jnp.sum(cnt, axis=0, keepdims=True)
    cnt_ref[...] = jnp.broadcast_to(carry_scr[0:1, :], cnt_ref.shape)
    info = jnp.where(lane == 0, i1, 0.0)
    info = jnp.where(lane == 1, i2, info)
    info = jnp.where(lane == 2, g1, info)
    info = jnp.where(lane == 3, g2, info)
    info = jnp.where(lane == 4, rank1, info)
    info = jnp.where(lane == 5, rank2, info)
    info_ref[...] = info


def _route(x, gain, router):
    m, d = x.shape
    ne = router.shape[1]
    rpad = jnp.pad(router, ((0, 0), (0, LANES - ne)))
    vm = 4 * TM * d * 4 + 2 * d * LANES * 4 + 4 * TM * LANES * 4 + TM * TM * 8 + 8 * TM * d * 4
    return pl.pallas_call(
        functools.partial(_route_kernel, n_experts=ne),
        grid=(m // TM,),
        in_specs=[
            pl.BlockSpec((TM, d), lambda i: (i, 0)),
            pl.BlockSpec((1, d), lambda i: (0, 0)),
            pl.BlockSpec((d, LANES), lambda i: (0, 0)),
        ],
        out_specs=[pl.BlockSpec((TM, d), lambda i: (i, 0)),
                   pl.BlockSpec((TM, LANES), lambda i: (i, 0)),
                   pl.BlockSpec((SUBLANES, LANES), lambda i: (0, 0))],
        out_shape=[jax.ShapeDtypeStruct((m, d), F32), jax.ShapeDtypeStruct((m, LANES), F32),
                   jax.ShapeDtypeStruct((SUBLANES, LANES), F32)],
        scratch_shapes=[pltpu.VMEM((SUBLANES, LANES), F32)],
        compiler_params=pltpu.CompilerParams(
            dimension_semantics=("arbitrary",), vmem_limit_bytes=_vmem_limit(vm)),
        name="route",
    )(x, gain.reshape(1, d), rpad)


def _row_copy(src, src_row, dst, dst_row, sem):
    return pltpu.make_async_copy(src.at[pl.ds(src_row, 1), :], dst.at[pl.ds(dst_row, 1), :], sem)


def _dispatch_kernel(p1_ref, p2_ref, hn_ref, init_ref, xs_ref, sem):
    del init_ref
    base = pl.program_id(0) * hn_ref.shape[0]
    tm = hn_ref.shape[0]

    def start(r, carry):
        _row_copy(hn_ref, r, xs_ref, p1_ref[base + r], sem).start()
        _row_copy(hn_ref, r, xs_ref, p2_ref[base + r], sem).start()
        return carry

    def wait(r, carry):
        _row_copy(hn_ref, r, xs_ref, p1_ref[base + r], sem).wait()
        _row_copy(hn_ref, r, xs_ref, p2_ref[base + r], sem).wait()
        return carry

    lax.fori_loop(0, tm, start, 0, unroll=ROW_DMA_UNROLL)
    lax.fori_loop(0, tm, wait, 0, unroll=ROW_DMA_UNROLL)


def _dispatch(hn, pos1, pos2, n_rows):
    m, d = hn.shape
    grid_spec = pltpu.PrefetchScalarGridSpec(
        num_scalar_prefetch=2,
        grid=(m // ROUTE_TM,),
        in_specs=[pl.BlockSpec((ROUTE_TM, d), lambda i, p1, p2: (i, 0)),
                  pl.BlockSpec(memory_space=pl.ANY)],
        out_specs=pl.BlockSpec(memory_space=pl.ANY),
        scratch_shapes=[pltpu.SemaphoreType.DMA(())],
    )
    return pl.pallas_call(
        _dispatch_kernel,
        grid_spec=grid_spec,
        out_shape=jax.ShapeDtypeStruct((n_rows, d), F32),
        input_output_aliases={3: 0},
        compiler_params=pltpu.CompilerParams(dimension_semantics=("arbitrary",)),
        name="dispatch",
    )(pos1, pos2, hn, jnp.zeros((n_rows, d), F32))


def _combine_kernel(p1_ref, p2_ref, x_ref, info_ref, gain_ref, ys_ref, o_ref, b1_scr, b2_scr, sem,
                    *, final_norm):
    tm = x_ref.shape[0]
    base = pl.program_id(0) * tm

    def start(r, carry):
        _row_copy(ys_ref, p1_ref[base + r], b1_scr, r, sem).start()
        _row_copy(ys_ref, p2_ref[base + r], b2_scr, r, sem).start()
        return carry

    def wait(r, carry):
        _row_copy(ys_ref, p1_ref[base + r], b1_scr, r, sem).wait()
        _row_copy(ys_ref, p2_ref[base + r], b2_scr, r, sem).wait()
        return carry

    lax.fori_loop(0, tm, start, 0, unroll=ROW_DMA_UNROLL)
    lax.fori_loop(0, tm, wait, 0, unroll=ROW_DMA_UNROLL)
    info = info_ref[...]
    e1, e2 = info[:, 0:1], info[:, 1:2]
    g1, g2 = info[:, 2:3], info[:, 3:4]
    y1 = g1 * b1_scr[...]
    y2 = g2 * b2_scr[...]
    lo = jnp.where(e1 < e2, y1, y2)
    hi = jnp.where(e1 < e2, y2, y1)
    out = x_ref[...] + (lo + hi)
    if final_norm:
        out = _rms(out, gain_ref[...])
    o_ref[...] = out


def _combine(x, info, ys, pos1, pos2, gain, *, final_norm):
    m, d = x.shape
    vm = 4 * ROUTE_TM * d * 4 + 2 * ROUTE_TM * LANES * 4 + 2 * ROUTE_TM * d * 4 + 6 * ROUTE_TM * d * 4
    grid_spec = pltpu.PrefetchScalarGridSpec(
        num_scalar_prefetch=2,
        grid=(m // ROUTE_TM,),
        in_specs=[
            pl.BlockSpec((ROUTE_TM, d), lambda i, p1, p2: (i, 0)),
            pl.BlockSpec((ROUTE_TM, LANES), lambda i, p1, p2: (i, 0)),
            pl.BlockSpec((1, d), lambda i, p1, p2: (0, 0)),
            pl.BlockSpec(memory_space=pl.ANY),
        ],
        out_specs=pl.BlockSpec((ROUTE_TM, d), lambda i, p1, p2: (i, 0)),
        scratch_shapes=[pltpu.VMEM((ROUTE_TM, d), F32), pltpu.VMEM((ROUTE_TM, d), F32),
                        pltpu.SemaphoreType.DMA(())],
    )
    return pl.pallas_call(
        functools.partial(_combine_kernel, final_norm=final_norm),
        grid_spec=grid_spec,
        out_shape=jax.ShapeDtypeStruct((m, d), F32),
        compiler_params=pltpu.CompilerParams(
            dimension_semantics=("arbitrary",), vmem_limit_bytes=_vmem_limit(vm)),
        name="combine_final" if final_norm else "combine",
    )(pos1, pos2, x, info, gain.reshape(1, d), ys)


def _moe_ffn(x, gain, router, wg, wu, wd, layer, out_gain, *, final_norm):
    m, d = x.shape
    ne = router.shape[1]
    wg, wu, wd = (w.reshape((-1,) + w.shape[2:]) for w in (wg, wu, wd))
    assert m % TM == 0 and m % ROUTE_TM == 0
    hn, info, cnt = _route(x, gain, router)
    counts = cnt[0, :ne].astype(jnp.int32)
    chunks_per = (counts + FFN_ROWS - 1) // FFN_ROWS
    chunk_end = jnp.cumsum(chunks_per)
    chunk_start = chunk_end - chunks_per
    share = (counts + jnp.maximum(chunks_per, 1) - 1) // jnp.maximum(chunks_per, 1)
    share = jnp.maximum((share + FFN_UNIT - 1) // FFN_UNIT * FFN_UNIT, FFN_UNIT)

    def position(e, rank):
        j = rank // share[e]
        return (chunk_start[e] + j) * FFN_ROWS + rank - j * share[e]

    pos1 = position(info[:, 0].astype(jnp.int32), info[:, 4].astype(jnp.int32))
    pos2 = position(info[:, 1].astype(jnp.int32), info[:, 5].astype(jnp.int32))
    nch = (m * TOP_K) // FFN_ROWS + ne
    cidx = jnp.arange(nch, dtype=jnp.int32)
    last = jnp.maximum(chunk_end[-1] - 1, 0)
    owner = jnp.sum(jnp.minimum(cidx, last)[:, None] >= chunk_end[None, :], axis=1).astype(jnp.int32)
    owner = jnp.minimum(owner, ne - 1)
    local = cidx - chunk_start[owner]
    nvalid = jnp.clip(counts[owner] - local * share[owner], 0, share[owner])
    nvalid = jnp.where(cidx < chunk_end[-1], nvalid, 0).astype(jnp.int32)
    units = (nvalid + FFN_UNIT - 1) // FFN_UNIT
    xs = _dispatch(hn, pos1, pos2, nch * FFN_ROWS)
    ys = _ffn(xs, gain, wg, wu, wd, owner + layer * ne, units, nvalid, dense=False)
    return _combine(x, info, ys, pos1, pos2, out_gain, final_norm=final_norm)


def kernel(x_prompt, x_sample, state_conv, state_pool, norm_mix, norm_ffn, final_norm, a_w_in, a_v_gain, a_w_s, a_b_s, a_w_out, b_w_in, b_conv, b_w_out, c_w_group, c_scale, ffn_w_gate, ffn_w_up, ffn_w_down, moe_router, moe_w_gate, moe_w_up, moe_w_down):
    batch, seq, d = x_prompt.shape
    slab, steps, _ = x_sample.shape
    depth = norm_mix.shape[0]
    assert depth % 2 == 0, "the final RMSNorm is fused into the routed combine kernel of the last layer"
    m_p = batch * seq
    geo = dict(m_p=m_p, slab=slab, steps=steps)

    def to_time_major(a):
        return a.transpose(1, 0, 2).reshape(a.shape[1] * slab, a.shape[2])

    def from_time_major(a, r):
        return a.reshape(r, slab, a.shape[-1]).transpose(1, 0, 2)

    def seq_tails(a, r):
        return jnp.stack([a[(b + 1) * seq - r:(b + 1) * seq] for b in range(batch)])

    a_w_out_b = a_w_out.astype(BF16)
    b_w_out_b = b_w_out.astype(BF16)
    x = jnp.concatenate([x_prompt.reshape(m_p, d), to_time_major(x_sample)], axis=0)
    chunk_v_s, conv_p, conv_s, pool_p, pool_s = [], [], [], [], []
    for i in range(depth):
        j, kind = divmod(i, 3)
        if kind == 0:
            z = _inproj_gelu(x, norm_mix[i], a_w_in, j)
            x, cv = _chunk_out(z, x, a_v_gain[j], a_w_s[j], a_b_s[j], a_w_out_b, j, **geo)
            chunk_v_s.append(from_time_major(cv, steps))
        elif kind == 1:
            hist = b_conv.shape[1] - 1
            bg, zc = _inproj_conv(x, norm_mix[i], b_w_in, j)
            x = _conv_out(bg, zc, x, to_time_major(state_conv[j]), b_conv[j], b_w_out_b, j, seq=seq, **geo)
            conv_p.append(seq_tails(zc, hist))
            conv_s.append(from_time_major(zc[m_p:], steps)[:, steps - hist:])
        else:
            nbuf = state_pool.shape[2]
            x, hn = _pool_mixer(x, norm_mix[i], to_time_major(state_pool[j]), c_w_group[j], c_scale[j],
                                seq=seq, **geo)
            pool_p.append(seq_tails(hn, nbuf))
            hc = jnp.concatenate([state_pool[j], from_time_major(hn[m_p:], steps)], axis=1)
            pool_s.append(hc[:, hc.shape[1] - nbuf:])
        f = i // 2
        if i % 2 == 0:
            x = _dense_ffn(x, norm_ffn[i], ffn_w_gate, ffn_w_up, ffn_w_down, f)
        else:
            x = _moe_ffn(x, norm_ffn[i], moe_router[f], moe_w_gate, moe_w_up, moe_w_down, f,
                         final_norm, final_norm=(i == depth - 1))
    y_prompt = x[:m_p].reshape(batch, seq, d)
    y_sample = from_time_major(x[m_p:], steps)
    return (y_prompt, y_sample, jnp.stack(chunk_v_s), jnp.stack(conv_p), jnp.stack(conv_s),
            jnp.stack(pool_p), jnp.stack(pool_s))
```

```python
import functools
import math

import jax
import jax.numpy as jnp
from jax import lax
from jax.experimental import pallas as pl
from jax.experimental.pallas import tpu as pltpu

F32 = jnp.float32
BF16 = jnp.bfloat16
EPS = 1e-6
PAST_LEN = 16384
POOL_WINDOWS = (2, 4, 8, 16)
TOP_K = 2

V7X_VMEM_BYTES = 64 * 1024 * 1024
SUBLANES = 8
LANES = 128

TM = 512
TN = 512
FFN_UNIT = 128
FFN_ROWS = 9 * FFN_UNIT
FFN_TF = 512
ROUTE_TM = 256
ROW_DMA_UNROLL = 8


def _vmem_limit(nbytes):
    return min(int(nbytes) + (6 << 20), V7X_VMEM_BYTES - (4 << 20))


def _rms(x, gain):
    ms = jnp.mean(x * x, axis=-1, keepdims=True)
    return x * lax.rsqrt(ms + EPS) * gain


def _gelu_tanh(x):
    c = math.sqrt(2.0 / math.pi)
    return 0.5 * x * (1.0 + jnp.tanh(c * (x + 0.044715 * (x * x * x))))


def _bdot(a, b):
    return jnp.dot(a, b, preferred_element_type=F32)


def _split_dot(a, b):
    ah = a.astype(BF16)
    al = (a - ah.astype(F32)).astype(BF16)
    bh = b.astype(BF16)
    bl = (b - bh.astype(F32)).astype(BF16)
    return _bdot(ah, bh) + _bdot(ah, bl) + _bdot(al, bh)


INPROJ_ROW_TILES = 8


def _inproj_rows(m):
    assert m % (INPROJ_ROW_TILES * 2 * SUBLANES) == 0
    return m // INPROJ_ROW_TILES


def _inproj_gelu_kernel(x_ref, g_ref, w_ref, o_ref, h_scr):
    @pl.when(pl.program_id(1) == 0)
    def _():
        h_scr[...] = _rms(x_ref[...], g_ref[...]).astype(BF16)

    z = _bdot(h_scr[...], w_ref[0].astype(BF16))
    o_ref[...] = _gelu_tanh(z)


def _inproj_gelu(x, gain, w, layer):
    m, d = x.shape
    n = w.shape[2]
    tm = _inproj_rows(m)
    vm = 2 * tm * d * 4 + tm * d * 2 + 2 * d * TN * 4 + d * TN * 2 + 4 * tm * TN * 4
    return pl.pallas_call(
        _inproj_gelu_kernel,
        grid=(m // tm, n // TN),
        in_specs=[
            pl.BlockSpec((tm, d), lambda i, j: (i, 0)),
            pl.BlockSpec((1, d), lambda i, j: (0, 0)),
            pl.BlockSpec((1, d, TN), lambda i, j: (layer, 0, j)),
        ],
        out_specs=pl.BlockSpec((tm, TN), lambda i, j: (i, j)),
        out_shape=jax.ShapeDtypeStruct((m, n), F32),
        scratch_shapes=[pltpu.VMEM((tm, d), BF16)],
        compiler_params=pltpu.CompilerParams(
            dimension_semantics=("arbitrary", "arbitrary"),
            vmem_limit_bytes=_vmem_limit(vm)),
        name="inproj_gelu",
    )(x, gain.reshape(1, d), w)


def _inproj_conv_kernel(x_ref, g_ref, wb_ref, wc_ref, wh_ref, ob_ref, oz_ref, h_scr):
    @pl.when(pl.program_id(1) == 0)
    def _():
        h_scr[...] = _rms(x_ref[...], g_ref[...]).astype(BF16)

    h = h_scr[...]
    ob_ref[...] = _bdot(h, wb_ref[0].astype(BF16))
    c = _bdot(h, wc_ref[0].astype(BF16))
    hin = _bdot(h, wh_ref[0].astype(BF16))
    oz_ref[...] = c * hin


def _inproj_conv(x, gain, w, layer):
    m, d = x.shape
    cd = w.shape[2] // 3
    tm = _inproj_rows(m)
    tn = TN // 2
    nb = cd // tn
    vm = 2 * tm * d * 4 + tm * d * 2 + 3 * (2 * d * tn * 4 + d * tn * 2) + 10 * tm * tn * 4
    out = jax.ShapeDtypeStruct((m, cd), F32)
    return pl.pallas_call(
        _inproj_conv_kernel,
        grid=(m // tm, nb),
        in_specs=[
            pl.BlockSpec((tm, d), lambda i, j: (i, 0)),
            pl.BlockSpec((1, d), lambda i, j: (0, 0)),
            pl.BlockSpec((1, d, tn), lambda i, j: (layer, 0, j)),
            pl.BlockSpec((1, d, tn), lambda i, j: (layer, 0, j + nb)),
            pl.BlockSpec((1, d, tn), lambda i, j: (layer, 0, j + 2 * nb)),
        ],
        out_specs=[pl.BlockSpec((tm, tn), lambda i, j: (i, j)),
                   pl.BlockSpec((tm, tn), lambda i, j: (i, j))],
        out_shape=[out, out],
        scratch_shapes=[pltpu.VMEM((tm, d), BF16)],
        compiler_params=pltpu.CompilerParams(
            dimension_semantics=("arbitrary", "arbitrary"),
            vmem_limit_bytes=_vmem_limit(vm)),
        name="inproj_conv",
    )(x, gain.reshape(1, d), w, w, w)


def _chunk_out_kernel(u_ref, v_ref, vg_ref, gw_ref, gb_ref, sw_ref, sb_ref, w_ref, x_ref,
                      o_ref, cv_ref, p_scr, *, np_tiles, chunk, groups, slab, steps):
    i = pl.program_id(0)
    gd = p_scr.shape[1] // groups

    @pl.when(i < np_tiles)
    def _prompt():
        row = lax.broadcasted_iota(jnp.int32, (chunk, chunk), 0)
        col = lax.broadcasted_iota(jnp.int32, (chunk, chunk), 1)
        causal = col <= row
        for s in range(p_scr.shape[0] // chunk):
            rs = slice(s * chunk, (s + 1) * chunk)
            vb = _rms(v_ref[rs, :], vg_ref[...]).astype(BF16)
            for h in range(groups):
                cs = slice(h * gd, (h + 1) * gd)
                wm = jnp.where(causal, gw_ref[h], 0.0).astype(BF16)
                mixed = _bdot(wm, vb[:, cs]) + gb_ref[:, h:h + 1]
                p_scr[rs, cs] = (u_ref[rs, cs] * mixed).astype(BF16)

    @pl.when(i >= np_tiles)
    def _sample():
        for t in range(steps):
            rs = slice(t * slab, (t + 1) * slab)
            cv_ref[rs, :] = _rms(v_ref[rs, :], vg_ref[...])
        for t in range(steps):
            rs = slice(t * slab, (t + 1) * slab)
            for h in range(groups):
                cs = slice(h * gd, (h + 1) * gd)
                mixed = sw_ref[t * steps:t * steps + 1, cs] * cv_ref[0:slab, cs]
                for s in range(1, t + 1):
                    k = t * steps + s
                    mixed = mixed + sw_ref[k:k + 1, cs] * cv_ref[s * slab:(s + 1) * slab, cs]
                mixed = mixed + sb_ref[t:t + 1, cs]
                p_scr[rs, cs] = (u_ref[rs, cs] * mixed).astype(BF16)

    o_ref[...] = x_ref[...] + _bdot(p_scr[...], w_ref[0])


def _chunk_out(z, x, v_gain, w_s, b_s, w_out, layer, *, m_p, slab, steps):
    m, d = x.shape
    a = z.shape[1] // 2
    groups, chunk = w_s.shape[0], w_s.shape[1]
    gd = a // groups
    assert m_p % TM == 0 and TM % chunk == 0 and m - m_p == TM == slab * steps
    assert steps <= chunk and slab % SUBLANES == 0 and w_out.dtype == BF16
    np_tiles = m_p // TM
    gb = b_s.T
    sw = jnp.repeat(w_s[:, :steps, :steps].transpose(1, 2, 0).reshape(steps * steps, groups), gd, axis=1)
    sb = jnp.repeat(b_s[:, :steps].T, gd, axis=1)
    vm = (4 * TM * a * 4 + TM * a * 2 + 2 * groups * chunk * chunk * 4 + a * d * 2
          + 4 * TM * d * 4 + TM * a * 4 + 4 * (steps * steps + steps) * a * 4 + 2 * TM * d * 4)
    once = pl.Buffered(1)
    kern = functools.partial(_chunk_out_kernel, np_tiles=np_tiles, chunk=chunk, groups=groups,
                             slab=slab, steps=steps)
    return pl.pallas_call(
        kern,
        grid=(m // TM,),
        in_specs=[
            pl.BlockSpec((TM, a), lambda i: (i, 0)),
            pl.BlockSpec((TM, a), lambda i: (i, 1)),
            pl.BlockSpec((1, a), lambda i: (0, 0)),
            pl.BlockSpec((groups, chunk, chunk), lambda i: (0, 0, 0)),
            pl.BlockSpec((chunk, groups), lambda i: (0, 0)),
            pl.BlockSpec((steps * steps, a), lambda i: (0, 0)),
            pl.BlockSpec((steps, a), lambda i: (0, 0)),
            pl.BlockSpec((1, a, d), lambda i: (layer, 0, 0), pipeline_mode=once),
            pl.BlockSpec((TM, d), lambda i: (i, 0)),
        ],
        out_specs=[pl.BlockSpec((TM, d), lambda i: (i, 0)),
                   pl.BlockSpec((TM, a), lambda i: (0, 0), pipeline_mode=once)],
        out_shape=[jax.ShapeDtypeStruct((m, d), F32), jax.ShapeDtypeStruct((TM, a), F32)],
        scratch_shapes=[pltpu.VMEM((TM, a), BF16)],
        compiler_params=pltpu.CompilerParams(
            dimension_semantics=("arbitrary",),
            vmem_limit_bytes=_vmem_limit(vm)),
        name="chunk_out",
    )(z, z, v_gain.reshape(1, a), w_s, gb, sw, sb, w_out, x)


def _conv_out_kernel(bg_ref, z_ref, halo_ref, st_ref, cw_ref, w_ref, x_ref, o_ref, p_scr,
                     *, np_tiles, tiles_per_seq, slab, steps):
    i = pl.program_id(0)
    width = cw_ref.shape[0]
    nhalo = halo_ref.shape[0]
    tm, cdim = p_scr.shape

    @pl.when(i < np_tiles)
    def _prompt():
        keep = (i % tiles_per_seq != 0).astype(F32)
        for c in range(cdim // TN):
            cs = slice(c * TN, (c + 1) * TN)
            ext = jnp.concatenate([halo_ref[:, cs] * keep, z_ref[:, cs]], axis=0)
            conv = cw_ref[0:1, cs] * pltpu.roll(ext, width - 1, axis=0)[nhalo:]
            for k in range(1, width):
                sh = width - 1 - k
                zk = pltpu.roll(ext, sh, axis=0)[nhalo:] if sh else z_ref[:, cs]
                conv = conv + cw_ref[k:k + 1, cs] * zk
            p_scr[:, cs] = (bg_ref[:, cs] * conv).astype(BF16)

    @pl.when(i >= np_tiles)
    def _sample():
        hist = width - 1

        def zrow(r, cs):
            if r < hist:
                return st_ref[r * slab:(r + 1) * slab, cs]
            return z_ref[(r - hist) * slab:(r - hist + 1) * slab, cs]

        for c in range(cdim // TN):
            cs = slice(c * TN, (c + 1) * TN)
            for t in range(steps):
                conv = cw_ref[0:1, cs] * zrow(t, cs)
                for k in range(1, width):
                    conv = conv + cw_ref[k:k + 1, cs] * zrow(t + k, cs)
                rs = slice(t * slab, (t + 1) * slab)
                p_scr[rs, cs] = (bg_ref[rs, cs] * conv).astype(BF16)

    o_ref[...] = x_ref[...] + _bdot(p_scr[...], w_ref[0])


def _conv_out(bg, zc, x, state_t, conv_w, w_out, layer, *, m_p, seq, slab, steps):
    m, d = x.shape
    cd = zc.shape[1]
    width = conv_w.shape[0]
    assert m - m_p == TM == slab * steps and seq % TM == 0 and width - 1 <= SUBLANES
    assert w_out.dtype == BF16
    np_tiles = m_p // TM
    halo_blocks = TM // SUBLANES
    vm = (4 * TM * cd * 4 + 2 * SUBLANES * cd * 4 + (width - 1) * slab * cd * 4 + TM * cd * 2
          + cd * d * 2 + 4 * TM * d * 4 + 2 * TM * d * 4 + 6 * TM * TN * 4)
    kern = functools.partial(_conv_out_kernel, np_tiles=np_tiles, tiles_per_seq=seq // TM,
                             slab=slab, steps=steps)
    once = pl.Buffered(1)
    return pl.pallas_call(
        kern,
        grid=(m // TM,),
        in_specs=[
            pl.BlockSpec((TM, cd), lambda i: (i, 0)),
            pl.BlockSpec((TM, cd), lambda i: (i, 0)),
            pl.BlockSpec((SUBLANES, cd), lambda i: (jnp.maximum(i * halo_blocks - 1, 0), 0)),
            pl.BlockSpec(((width - 1) * slab, cd), lambda i: (0, 0), pipeline_mode=once),
            pl.BlockSpec((width, cd), lambda i: (0, 0)),
            pl.BlockSpec((1, cd, d), lambda i: (layer, 0, 0), pipeline_mode=once),
            pl.BlockSpec((TM, d), lambda i: (i, 0)),
        ],
        out_specs=pl.BlockSpec((TM, d), lambda i: (i, 0)),
        out_shape=jax.ShapeDtypeStruct((m, d), F32),
        scratch_shapes=[pltpu.VMEM((TM, cd), BF16)],
        compiler_params=pltpu.CompilerParams(
            dimension_semantics=("arbitrary",),
            vmem_limit_bytes=_vmem_limit(vm)),
        name="conv_out",
    )(bg, zc, zc, state_t, conv_w, w_out, x)


def _pool_kernel(xf_ref, hf_ref, xg_ref, hg_ref, buf_ref, gf_ref, gg_ref, w_ref, sc_ref,
                 o_ref, hn_ref, r_scr, *, np_tiles, tiles_per_seq, slab, steps, windows, pos0):
    i = pl.program_id(0)
    g = pl.program_id(1)
    tm = xg_ref.shape[0]
    nhalo = hg_ref.shape[0]
    nbuf = buf_ref.shape[0] // slab

    @pl.when(g == 0)
    def _():
        xf = xf_ref[...]
        r_scr[nhalo:, :] = lax.rsqrt(jnp.mean(xf * xf, axis=-1, keepdims=True) + EPS)
        hf = hf_ref[...]
        r_scr[:nhalo, :] = lax.rsqrt(jnp.mean(hf * hf, axis=-1, keepdims=True) + EPS)

    h = xg_ref[...] * r_scr[nhalo:, :] * gg_ref[...]
    hn_ref[...] = h

    def finish(pooled):
        diff = (pooled - h).astype(BF16)
        y = _bdot(diff, w_ref[0].astype(BF16)) * sc_ref[...]
        o_ref[...] = xg_ref[...] + y

    for gi, win in enumerate(windows):
        @pl.when(jnp.logical_and(g == gi, i < np_tiles))
        def _prompt(win=win):
            seq_tile = i % tiles_per_seq
            keep = (seq_tile != 0).astype(F32)
            hh = hg_ref[...] * r_scr[:nhalo, :] * gg_ref[...] * keep
            s = jnp.concatenate([hh, h], axis=0)
            k = 1
            while k < win:
                s = s + pltpu.roll(s, k, axis=0)
                k *= 2
            pos = seq_tile * tm + lax.broadcasted_iota(jnp.int32, (tm, 1), 0)
            count = jnp.minimum(win, pos + 1).astype(F32)
            finish(s[nhalo:] / count)

        @pl.when(jnp.logical_and(g == gi, i >= np_tiles))
        def _sample(win=win):
            def hrow(r):
                if r < nbuf:
                    return buf_ref[r * slab:(r + 1) * slab, :]
                return h[(r - nbuf) * slab:(r - nbuf + 1) * slab]

            rows = []
            for t in range(steps):
                acc = hrow(nbuf + t - win + 1)
                for r in range(nbuf + t - win + 2, nbuf + t + 1):
                    acc = acc + hrow(r)
                rows.append(acc / float(min(win, pos0 + t + 1)))
            finish(jnp.concatenate(rows, axis=0))


def _pool_mixer(x, gain, buf_t, w_group, scale, *, m_p, seq, slab, steps):
    m, d = x.shape
    ng, gdim = w_group.shape[0], w_group.shape[1]
    nhalo = 2 * SUBLANES
    assert ng == len(POOL_WINDOWS) and max(POOL_WINDOWS) <= nhalo and gdim * ng == d
    assert m - m_p == TM == slab * steps and seq % TM == 0
    assert buf_t.shape[0] // slab >= max(POOL_WINDOWS) - 1
    np_tiles = m_p // TM
    halo_blocks = TM // nhalo
    nbuf_rows = buf_t.shape[0]
    vm = (2 * TM * d * 4 + 2 * nhalo * d * 4 + 2 * TM * gdim * 4 + 2 * nhalo * gdim * 4
          + 2 * nbuf_rows * gdim * 4 + 2 * gdim * gdim * 4 + gdim * gdim * 2
          + 4 * TM * gdim * 4 + (TM + nhalo) * LANES * 4 + 10 * TM * gdim * 4)
    kern = functools.partial(_pool_kernel, np_tiles=np_tiles, tiles_per_seq=seq // TM, slab=slab,
                             steps=steps, windows=POOL_WINDOWS, pos0=PAST_LEN)
    halo_idx = lambda i: jnp.maximum(i * halo_blocks - 1, 0)
    return pl.pallas_call(
        kern,
        grid=(m // TM, ng),
        in_specs=[
            pl.BlockSpec((TM, d), lambda i, g: (i, 0)),
            pl.BlockSpec((nhalo, d), lambda i, g: (halo_idx(i), 0)),
            pl.BlockSpec((TM, gdim), lambda i, g: (i, g)),
            pl.BlockSpec((nhalo, gdim), lambda i, g: (halo_idx(i), g)),
            pl.BlockSpec((nbuf_rows, gdim), lambda i, g: (0, jnp.where(i >= np_tiles, g, 0))),
            pl.BlockSpec((1, d), lambda i, g: (0, 0)),
            pl.BlockSpec((1, gdim), lambda i, g: (0, g)),
            pl.BlockSpec((1, gdim, gdim), lambda i, g: (g, 0, 0)),
            pl.BlockSpec((1, gdim), lambda i, g: (0, g)),
        ],
        out_specs=[pl.BlockSpec((TM, gdim), lambda i, g: (i, g)),
                   pl.BlockSpec((TM, gdim), lambda i, g: (i, g))],
        out_shape=[jax.ShapeDtypeStruct((m, d), F32), jax.ShapeDtypeStruct((m, d), F32)],
        scratch_shapes=[pltpu.VMEM((TM + nhalo, 1), F32)],
        compiler_params=pltpu.CompilerParams(
            dimension_semantics=("arbitrary", "arbitrary"),
            vmem_limit_bytes=_vmem_limit(vm)),
        name="pool_mixer",
    )(x, x, x, x, buf_t, gain.reshape(1, d), gain.reshape(1, d), w_group, scale.reshape(1, d))


def _ffn_kernel(ce_ref, nu_ref, nv_ref, xb_ref, x_ref, gain_ref, wg_hbm, wu_hbm, wd_hbm, o_ref,
                xb_scr, wg_buf, wu_buf, wd_buf, sems, *, dense, nf):
    del xb_ref
    c = pl.program_id(0)
    nch = pl.num_programs(0)
    rows = x_ref.shape[0]
    nvalid = nv_ref[c]
    units = nu_ref[c]

    def weight_copies(chunk, f, slot):
        e = ce_ref[chunk]
        cols = pl.ds(pl.multiple_of(f * FFN_TF, FFN_TF), FFN_TF)
        return (pltpu.make_async_copy(wg_hbm.at[e, :, cols], wg_buf.at[slot], sems.at[0, slot]),
                pltpu.make_async_copy(wu_hbm.at[e, :, cols], wu_buf.at[slot], sems.at[1, slot]),
                pltpu.make_async_copy(wd_hbm.at[e, cols, :], wd_buf.at[slot], sems.at[2, slot]))

    def start_weights(chunk, f, slot):
        for copy in weight_copies(chunk, f, slot):
            copy.start()

    def wait_weights(chunk, f, slot):
        for copy in weight_copies(chunk, f, slot):
            copy.wait()

    @pl.when(jnp.logical_and(c == 0, units > 0))
    def _prime():
        start_weights(0, 0, 0)

    for s in range(rows // FFN_UNIT):
        rs = slice(s * FFN_UNIT, (s + 1) * FFN_UNIT)
        if dense:
            x = x_ref[rs, :]
            live = (s * FFN_UNIT + lax.broadcasted_iota(jnp.int32, (FFN_UNIT, 1), 0)) < nvalid
            o_ref[rs, :] = x
            xb_scr[rs, :] = jnp.where(live, _rms(x, gain_ref[...]), 0.0).astype(BF16)
        else:
            o_ref[rs, :] = jnp.zeros((FFN_UNIT, o_ref.shape[1]), F32)
            xb_scr[rs, :] = x_ref[rs, :].astype(BF16)

    def sub_tile(slot, start, nrows):
        rs = pl.ds(pl.multiple_of(start, FFN_UNIT), nrows)
        xs = xb_scr[rs, :]
        gate = _bdot(xs, wg_buf[slot].astype(BF16))
        up = _bdot(xs, wu_buf[slot].astype(BF16))
        act = (gate * jax.nn.sigmoid(gate) * up).astype(BF16)
        o_ref[rs, :] += _bdot(act, wd_buf[slot].astype(BF16))

    def swiglu_block(slot):
        full = rows // FFN_UNIT

        @pl.when(units == full)
        def _full():
            sub_tile(slot, 0, rows)

        @pl.when(units < full)
        def _partial():
            nquad = lax.shift_right_logical(units, 2)

            def quad(s, carry):
                sub_tile(slot, s * (4 * FFN_UNIT), 4 * FFN_UNIT)
                return carry

            lax.fori_loop(0, nquad, quad, 0)
            has_pair = lax.bitwise_and(units, 2)

            @pl.when(has_pair != 0)
            def _pair():
                sub_tile(slot, nquad * (4 * FFN_UNIT), 2 * FFN_UNIT)

            @pl.when(lax.bitwise_and(units, 1) != 0)
            def _single():
                sub_tile(slot, nquad * (4 * FFN_UNIT) + has_pair * FFN_UNIT, FFN_UNIT)

    @pl.when(units > 0)
    def _work():
        nxt = jnp.minimum(c + 1, nch - 1)
        next_live = jnp.logical_and(c + 1 < nch, nu_ref[nxt] > 0)

        def block(f, carry):
            slot = lax.bitwise_and(f, 1)
            wait_weights(c, f, slot)

            @pl.when(f + 1 < nf)
            def _():
                start_weights(c, f + 1, 1 - slot)

            @pl.when(jnp.logical_and(f + 1 == nf, next_live))
            def _():
                start_weights(nxt, 0, 1 - slot)

            swiglu_block(slot)
            return carry

        lax.fori_loop(0, nf, block, 0)


def _ffn(x, gain, wg, wu, wd, chunk_expert, chunk_units, chunk_nvalid, *, dense):
    m, d = x.shape
    dff = wg.shape[-1]
    nch = chunk_expert.shape[0]
    nf = dff // FFN_TF
    assert dff % FFN_TF == 0 and nf % 2 == 0
    last_live = jnp.maximum(jnp.sum(chunk_units > 0) - 1, 0).astype(jnp.int32)
    x_block = jnp.minimum(jnp.arange(nch, dtype=jnp.int32), last_live)
    vm = (2 * FFN_ROWS * d * 4 + FFN_ROWS * d * 2 + 3 * 2 * d * FFN_TF * 4
          + 3 * d * FFN_TF * 2 + 16 * FFN_UNIT * FFN_TF * 4)
    once = pl.Buffered(1)
    grid_spec = pltpu.PrefetchScalarGridSpec(
        num_scalar_prefetch=4,
        grid=(nch,),
        in_specs=[
            pl.BlockSpec((FFN_ROWS, d), lambda c, ce, nu, nv, xb: (xb[c], 0), pipeline_mode=once),
            pl.BlockSpec((1, d), lambda c, ce, nu, nv, xb: (0, 0)),
            pl.BlockSpec(memory_space=pl.ANY),
            pl.BlockSpec(memory_space=pl.ANY),
            pl.BlockSpec(memory_space=pl.ANY),
        ],
        out_specs=pl.BlockSpec((FFN_ROWS, d), lambda c, ce, nu, nv, xb: (c, 0), pipeline_mode=once),
        scratch_shapes=[pltpu.VMEM((FFN_ROWS, d), BF16),
                        pltpu.VMEM((2, d, FFN_TF), F32), pltpu.VMEM((2, d, FFN_TF), F32),
                        pltpu.VMEM((2, FFN_TF, d), F32), pltpu.SemaphoreType.DMA((3, 2))],
    )
    return pl.pallas_call(
        functools.partial(_ffn_kernel, dense=dense, nf=nf),
        grid_spec=grid_spec,
        out_shape=jax.ShapeDtypeStruct((m, d), F32),
        compiler_params=pltpu.CompilerParams(
            dimension_semantics=("arbitrary",),
            vmem_limit_bytes=_vmem_limit(vm)),
        name="ffn_dense" if dense else "ffn_experts",
    )(chunk_expert, chunk_units, chunk_nvalid, x_block, x, gain.reshape(1, d), wg, wu, wd)


def _dense_ffn(x, gain, wg, wu, wd, layer):
    m = x.shape[0]
    nch = pl.cdiv(m, FFN_ROWS)
    nvalid = jnp.minimum(FFN_ROWS, m - FFN_ROWS * jnp.arange(nch, dtype=jnp.int32))
    units = (nvalid + FFN_UNIT - 1) // FFN_UNIT
    which = jnp.full((nch,), layer, jnp.int32)
    return _ffn(x, gain, wg, wu, wd, which, units, nvalid, dense=True)


def _route_kernel(x_ref, g_ref, r_ref, hn_ref, info_ref, cnt_ref, carry_scr, *, n_experts):
    i = pl.program_id(0)
    tm = x_ref.shape[0]

    @pl.when(i == 0)
    def _():
        carry_scr[...] = jnp.zeros_like(carry_scr)

    hn = _rms(x_ref[...], g_ref[...])
    hn_ref[...] = hn
    lane = lax.broadcasted_iota(jnp.int32, (tm, LANES), 1).astype(F32)
    logits = jnp.where(lane < n_experts, _split_dot(hn, r_ref[...]), -jnp.inf)
    m1 = jnp.max(logits, axis=-1, keepdims=True)
    i1 = jnp.min(jnp.where(logits == m1, lane, float(LANES)), axis=-1, keepdims=True)
    rest = jnp.where(lane == i1, -jnp.inf, logits)
    m2 = jnp.max(rest, axis=-1, keepdims=True)
    i2 = jnp.min(jnp.where(rest == m2, lane, float(LANES)), axis=-1, keepdims=True)
    e = jnp.exp(m2 - m1)
    g1 = 1.0 / (1.0 + e)
    g2 = e / (1.0 + e)
    oh1 = (lane == i1).astype(F32)
    oh2 = (lane == i2).astype(F32)
    cnt = oh1 + oh2
    row = lax.broadcasted_iota(jnp.int32, (tm, tm), 0)
    col = lax.broadcasted_iota(jnp.int32, (tm, tm), 1)
    before = (col < row).astype(BF16)
    ranks = _bdot(before, cnt.astype(BF16)) + carry_scr[0:1, :]
    rank1 = jnp.sum(ranks * oh1, axis=-1, keepdims=True)
    rank2 = jnp.sum(ranks * oh2, axis=-1, keepdims=True)
    carry_scr[0:1, :] = carry_scr[0:1, :] + jnp.sum(cnt, axis=0, keepdims=True)
    cnt_ref[...] = jnp.broadcast_to(carry_scr[0:1, :], cnt_ref.shape)
    info = jnp.where(lane == 0, i1, 0.0)
    info = jnp.where(lane == 1, i2, info)
    info = jnp.where(lane == 2, g1, info)
    info = jnp.where(lane == 3, g2, info)
    info = jnp.where(lane == 4, rank1, info)
    info = jnp.where(lane == 5, rank2, info)
    info_ref[...] = info


def _route(x, gain, router):
    m, d = x.shape
    ne = router.shape[1]
    rpad = jnp.pad(router, ((0, 0), (0, LANES - ne)))
    vm = 4 * TM * d * 4 + 2 * d * LANES * 4 + 4 * TM * LANES * 4 + TM * TM * 8 + 8 * TM * d * 4
    return pl.pallas_call(
        functools.partial(_route_kernel, n_experts=ne),
        grid=(m // TM,),
        in_specs=[
            pl.BlockSpec((TM, d), lambda i: (i, 0)),
            pl.BlockSpec((1, d), lambda i: (0, 0)),
            pl.BlockSpec((d, LANES), lambda i: (0, 0)),
        ],
        out_specs=[pl.BlockSpec((TM, d), lambda i: (i, 0)),
                   pl.BlockSpec((TM, LANES), lambda i: (i, 0)),
                   pl.BlockSpec((SUBLANES, LANES), lambda i: (0, 0))],
        out_shape=[jax.ShapeDtypeStruct((m, d), F32), jax.ShapeDtypeStruct((m, LANES), F32),
                   jax.ShapeDtypeStruct((SUBLANES, LANES), F32)],
        scratch_shapes=[pltpu.VMEM((SUBLANES, LANES), F32)],
        compiler_params=pltpu.CompilerParams(
            dimension_semantics=("arbitrary",), vmem_limit_bytes=_vmem_limit(vm)),
        name="route",
    )(x, gain.reshape(1, d), rpad)


def _row_copy(src, src_row, dst, dst_row, sem):
    return pltpu.make_async_copy(src.at[pl.ds(src_row, 1), :], dst.at[pl.ds(dst_row, 1), :], sem)


def _dispatch_kernel(p1_ref, p2_ref, hn_ref, init_ref, xs_ref, sem):
    del init_ref
    base = pl.program_id(0) * hn_ref.shape[0]
    tm = hn_ref.shape[0]

    def start(r, carry):
        _row_copy(hn_ref, r, xs_ref, p1_ref[base + r], sem).start()
        _row_copy(hn_ref, r, xs_ref, p2_ref[base + r], sem).start()
        return carry

    def wait(r, carry):
        _row_copy(hn_ref, r, xs_ref, p1_ref[base + r], sem).wait()
        _row_copy(hn_ref, r, xs_ref, p2_ref[base + r], sem).wait()
        return carry

    lax.fori_loop(0, tm, start, 0, unroll=ROW_DMA_UNROLL)
    lax.fori_loop(0, tm, wait, 0, unroll=ROW_DMA_UNROLL)


def _dispatch(hn, pos1, pos2, n_rows):
    m, d = hn.shape
    grid_spec = pltpu.PrefetchScalarGridSpec(
        num_scalar_prefetch=2,
        grid=(m // ROUTE_TM,),
        in_specs=[pl.BlockSpec((ROUTE_TM, d), lambda i, p1, p2: (i, 0)),
                  pl.BlockSpec(memory_space=pl.ANY)],
        out_specs=pl.BlockSpec(memory_space=pl.ANY),
        scratch_shapes=[pltpu.SemaphoreType.DMA(())],
    )
    return pl.pallas_call(
        _dispatch_kernel,
        grid_spec=grid_spec,
        out_shape=jax.ShapeDtypeStruct((n_rows, d), F32),
        input_output_aliases={3: 0},
        compiler_params=pltpu.CompilerParams(dimension_semantics=("arbitrary",)),
        name="dispatch",
    )(pos1, pos2, hn, jnp.zeros((n_rows, d), F32))


def _combine_kernel(p1_ref, p2_ref, x_ref, info_ref, gain_ref, ys_ref, o_ref, b1_scr, b2_scr, sems,
                    *, final_norm):
    tm = x_ref.shape[0]
    i = pl.program_id(0)
    slot = lax.bitwise_and(i, 1)

    def gather(tile, sl, go):
        def row(r, carry):
            t = tile * tm + r
            for pos_ref, buf in ((p1_ref, b1_scr), (p2_ref, b2_scr)):
                copy = _row_copy(ys_ref, pos_ref[t], buf.at[sl], r, sems.at[sl])
                copy.start() if go else copy.wait()
            return carry

        lax.fori_loop(0, tm, row, 0, unroll=ROW_DMA_UNROLL)

    @pl.when(i == 0)
    def _():
        gather(0, 0, True)

    @pl.when(i + 1 < pl.num_programs(0))
    def _():
        gather(i + 1, 1 - slot, True)

    gather(i, slot, False)
    info = info_ref[...]
    e1, e2 = info[:, 0:1], info[:, 1:2]
    g1, g2 = info[:, 2:3], info[:, 3:4]
    y1 = g1 * b1_scr[slot]
    y2 = g2 * b2_scr[slot]
    lo = jnp.where(e1 < e2, y1, y2)
    hi = jnp.where(e1 < e2, y2, y1)
    out = x_ref[...] + (lo + hi)
    if final_norm:
        out = _rms(out, gain_ref[...])
    o_ref[...] = out


def _combine(x, info, ys, pos1, pos2, gain, *, final_norm):
    m, d = x.shape
    vm = 4 * ROUTE_TM * d * 4 + 2 * ROUTE_TM * LANES * 4 + 2 * ROUTE_TM * d * 4 + 6 * ROUTE_TM * d * 4
    grid_spec = pltpu.PrefetchScalarGridSpec(
        num_scalar_prefetch=2,
        grid=(m // ROUTE_TM,),
        in_specs=[
            pl.BlockSpec((ROUTE_TM, d), lambda i, p1, p2: (i, 0)),
            pl.BlockSpec((ROUTE_TM, LANES), lambda i, p1, p2: (i, 0)),
            pl.BlockSpec((1, d), lambda i, p1, p2: (0, 0)),
            pl.BlockSpec(memory_space=pl.ANY),
        ],
        out_specs=pl.BlockSpec((ROUTE_TM, d), lambda i, p1, p2: (i, 0)),
        scratch_shapes=[pltpu.VMEM((2, ROUTE_TM, d), F32), pltpu.VMEM((2, ROUTE_TM, d), F32),
                        pltpu.SemaphoreType.DMA((2,))],
    )
    return pl.pallas_call(
        functools.partial(_combine_kernel, final_norm=final_norm),
        grid_spec=grid_spec,
        out_shape=jax.ShapeDtypeStruct((m, d), F32),
        compiler_params=pltpu.CompilerParams(
            dimension_semantics=("arbitrary",), vmem_limit_bytes=_vmem_limit(vm)),
        name="combine_final" if final_norm else "combine",
    )(pos1, pos2, x, info, gain.reshape(1, d), ys)


def _moe_ffn(x, gain, router, wg, wu, wd, layer, out_gain, *, final_norm):
    m, d = x.shape
    ne = router.shape[1]
    wg, wu, wd = (w.reshape((-1,) + w.shape[2:]) for w in (wg, wu, wd))
    assert m % TM == 0 and m % ROUTE_TM == 0
    hn, info, cnt = _route(x, gain, router)
    counts = cnt[0, :ne].astype(jnp.int32)
    chunks_per = (counts + FFN_ROWS - 1) // FFN_ROWS
    chunk_end = jnp.cumsum(chunks_per)
    chunk_start = chunk_end - chunks_per
    share = (counts + jnp.maximum(chunks_per, 1) - 1) // jnp.maximum(chunks_per, 1)
    share = jnp.maximum((share + FFN_UNIT - 1) // FFN_UNIT * FFN_UNIT, FFN_UNIT)

    def position(e, rank):
        j = rank // share[e]
        return (chunk_start[e] + j) * FFN_ROWS + rank - j * share[e]

    pos1 = position(info[:, 0].astype(jnp.int32), info[:, 4].astype(jnp.int32))
    pos2 = position(info[:, 1].astype(jnp.int32), info[:, 5].astype(jnp.int32))
    nch = (m * TOP_K) // FFN_ROWS + ne
    cidx = jnp.arange(nch, dtype=jnp.int32)
    last = jnp.maximum(chunk_end[-1] - 1, 0)
    owner = jnp.sum(jnp.minimum(cidx, last)[:, None] >= chunk_end[None, :], axis=1).astype(jnp.int32)
    owner = jnp.minimum(owner, ne - 1)
    local = cidx - chunk_start[owner]
    nvalid = jnp.clip(counts[owner] - local * share[owner], 0, share[owner])
    nvalid = jnp.where(cidx < chunk_end[-1], nvalid, 0).astype(jnp.int32)
    units = (nvalid + FFN_UNIT - 1) // FFN_UNIT
    xs = _dispatch(hn, pos1, pos2, nch * FFN_ROWS)
    ys = _ffn(xs, gain, wg, wu, wd, owner + layer * ne, units, nvalid, dense=False)
    return _combine(x, info, ys, pos1, pos2, out_gain, final_norm=final_norm)


def kernel(x_prompt, x_sample, state_conv, state_pool, norm_mix, norm_ffn, final_norm, a_w_in, a_v_gain, a_w_s, a_b_s, a_w_out, b_w_in, b_conv, b_w_out, c_w_group, c_scale, ffn_w_gate, ffn_w_up, ffn_w_down, moe_router, moe_w_gate, moe_w_up, moe_w_down):
    batch, seq, d = x_prompt.shape
    slab, steps, _ = x_sample.shape
    depth = norm_mix.shape[0]
    assert depth % 2 == 0, "the final RMSNorm is fused into the routed combine kernel of the last layer"
    m_p = batch * seq
    geo = dict(m_p=m_p, slab=slab, steps=steps)

    def to_time_major(a):
        return a.transpose(1, 0, 2).reshape(a.shape[1] * slab, a.shape[2])

    def from_time_major(a, r):
        return a.reshape(r, slab, a.shape[-1]).transpose(1, 0, 2)

    def seq_tails(a, r):
        return jnp.stack([a[(b + 1) * seq - r:(b + 1) * seq] for b in range(batch)])

    a_w_out_b = a_w_out.astype(BF16)
    b_w_out_b = b_w_out.astype(BF16)
    x = jnp.concatenate([x_prompt.reshape(m_p, d), to_time_major(x_sample)], axis=0)
    chunk_v_s, conv_p, conv_s, pool_p, pool_s = [], [], [], [], []
    for i in range(depth):
        j, kind = divmod(i, 3)
        if kind == 0:
            z = _inproj_gelu(x, norm_mix[i], a_w_in, j)
            x, cv = _chunk_out(z, x, a_v_gain[j], a_w_s[j], a_b_s[j], a_w_out_b, j, **geo)
            chunk_v_s.append(from_time_major(cv, steps))
        elif kind == 1:
            hist = b_conv.shape[1] - 1
            bg, zc = _inproj_conv(x, norm_mix[i], b_w_in, j)
            x = _conv_out(bg, zc, x, to_time_major(state_conv[j]), b_conv[j], b_w_out_b, j, seq=seq, **geo)
            conv_p.append(seq_tails(zc, hist))
            conv_s.append(from_time_major(zc[m_p:], steps)[:, steps - hist:])
        else:
            nbuf = state_pool.shape[2]
            x, hn = _pool_mixer(x, norm_mix[i], to_time_major(state_pool[j]), c_w_group[j], c_scale[j],
                                seq=seq, **geo)
            pool_p.append(seq_tails(hn, nbuf))
            hc = jnp.concatenate([state_pool[j], from_time_major(hn[m_p:], steps)], axis=1)
            pool_s.append(hc[:, hc.shape[1] - nbuf:])
        f = i // 2
        if i % 2 == 0:
            x = _dense_ffn(x, norm_ffn[i], ffn_w_gate, ffn_w_up, ffn_w_down, f)
        else:
            x = _moe_ffn(x, norm_ffn[i], moe_router[f], moe_w_gate, moe_w_up, moe_w_down, f,
                         final_norm, final_norm=(i == depth - 1))
    y_prompt = x[:m_p].reshape(batch, seq, d)
    y_sample = from_time_major(x[m_p:], steps)
    return (y_prompt, y_sample, jnp.stack(chunk_v_s), jnp.stack(conv_p), jnp.stack(conv_s),
            jnp.stack(pool_p), jnp.stack(pool_s))
```

```python
import functools
import math

import jax
import jax.numpy as jnp
from jax import lax
from jax.experimental import pallas as pl
from jax.experimental.pallas import tpu as pltpu

F32 = jnp.float32
BF16 = jnp.bfloat16
EPS = 1e-6
PAST_LEN = 16384
POOL_WINDOWS = (2, 4, 8, 16)
TOP_K = 2

V7X_VMEM_BYTES = 64 * 1024 * 1024
SUBLANES = 8
LANES = 128

TM = 512
TN = 512
FFN_UNIT = 128
FFN_ROWS = 9 * FFN_UNIT
FFN_TF = 512
ROUTE_TM = 256
ROW_DMA_UNROLL = 8


def _vmem_limit(nbytes):
    return min(int(nbytes) + (6 << 20), V7X_VMEM_BYTES - (4 << 20))


def _rms(x, gain):
    ms = jnp.mean(x * x, axis=-1, keepdims=True)
    return x * lax.rsqrt(ms + EPS) * gain


def _gelu_tanh(x):
    c = math.sqrt(2.0 / math.pi)
    return 0.5 * x * (1.0 + jnp.tanh(c * (x + 0.044715 * (x * x * x))))


def _bdot(a, b):
    return jnp.dot(a, b, preferred_element_type=F32)


def _split_dot(a, b):
    ah = a.astype(BF16)
    al = (a - ah.astype(F32)).astype(BF16)
    bh = b.astype(BF16)
    bl = (b - bh.astype(F32)).astype(BF16)
    return _bdot(ah, bh) + _bdot(ah, bl) + _bdot(al, bh)


INPROJ_ROW_TILES = 8


def _inproj_rows(m):
    assert m % (INPROJ_ROW_TILES * 2 * SUBLANES) == 0
    return m // INPROJ_ROW_TILES


def _inproj_gelu_kernel(x_ref, g_ref, w_ref, o_ref, h_scr):
    @pl.when(pl.program_id(1) == 0)
    def _():
        h_scr[...] = _rms(x_ref[...], g_ref[...]).astype(BF16)

    z = _bdot(h_scr[...], w_ref[0].astype(BF16))
    o_ref[...] = _gelu_tanh(z)


def _inproj_gelu(x, gain, w, layer):
    m, d = x.shape
    n = w.shape[2]
    tm = _inproj_rows(m)
    tn = 2 * TN
    vm = 2 * tm * d * 4 + tm * d * 2 + 2 * d * tn * 4 + d * tn * 2 + 4 * tm * tn * 4
    return pl.pallas_call(
        _inproj_gelu_kernel,
        grid=(m // tm, n // tn),
        in_specs=[
            pl.BlockSpec((tm, d), lambda i, j: (i, 0)),
            pl.BlockSpec((1, d), lambda i, j: (0, 0)),
            pl.BlockSpec((1, d, tn), lambda i, j: (layer, 0, j)),
        ],
        out_specs=pl.BlockSpec((tm, tn), lambda i, j: (i, j)),
        out_shape=jax.ShapeDtypeStruct((m, n), F32),
        scratch_shapes=[pltpu.VMEM((tm, d), BF16)],
        compiler_params=pltpu.CompilerParams(
            dimension_semantics=("arbitrary", "arbitrary"),
            vmem_limit_bytes=_vmem_limit(vm)),
        name="inproj_gelu",
    )(x, gain.reshape(1, d), w)


def _inproj_conv_kernel(x_ref, g_ref, wb_ref, wc_ref, wh_ref, ob_ref, oz_ref, h_scr):
    @pl.when(pl.program_id(1) == 0)
    def _():
        h_scr[...] = _rms(x_ref[...], g_ref[...]).astype(BF16)

    h = h_scr[...]
    ob_ref[...] = _bdot(h, wb_ref[0].astype(BF16))
    c = _bdot(h, wc_ref[0].astype(BF16))
    hin = _bdot(h, wh_ref[0].astype(BF16))
    oz_ref[...] = c * hin


def _inproj_conv(x, gain, w, layer):
    m, d = x.shape
    cd = w.shape[2] // 3
    tm = _inproj_rows(m)
    tn = TN // 2
    nb = cd // tn
    vm = 2 * tm * d * 4 + tm * d * 2 + 3 * (2 * d * tn * 4 + d * tn * 2) + 10 * tm * tn * 4
    out = jax.ShapeDtypeStruct((m, cd), F32)
    return pl.pallas_call(
        _inproj_conv_kernel,
        grid=(m // tm, nb),
        in_specs=[
            pl.BlockSpec((tm, d), lambda i, j: (i, 0)),
            pl.BlockSpec((1, d), lambda i, j: (0, 0)),
            pl.BlockSpec((1, d, tn), lambda i, j: (layer, 0, j)),
            pl.BlockSpec((1, d, tn), lambda i, j: (layer, 0, j + nb)),
            pl.BlockSpec((1, d, tn), lambda i, j: (layer, 0, j + 2 * nb)),
        ],
        out_specs=[pl.BlockSpec((tm, tn), lambda i, j: (i, j)),
                   pl.BlockSpec((tm, tn), lambda i, j: (i, j))],
        out_shape=[out, out],
        scratch_shapes=[pltpu.VMEM((tm, d), BF16)],
        compiler_params=pltpu.CompilerParams(
            dimension_semantics=("arbitrary", "arbitrary"),
            vmem_limit_bytes=_vmem_limit(vm)),
        name="inproj_conv",
    )(x, gain.reshape(1, d), w, w, w)


def _chunk_out_kernel(u_ref, v_ref, vg_ref, gw_ref, gb_ref, sw_ref, sb_ref, w_ref, x_ref,
                      o_ref, cv_ref, p_scr, *, np_tiles, chunk, groups, slab, steps):
    i = pl.program_id(0)
    gd = p_scr.shape[1] // groups

    @pl.when(i < np_tiles)
    def _prompt():
        row = lax.broadcasted_iota(jnp.int32, (chunk, chunk), 0)
        col = lax.broadcasted_iota(jnp.int32, (chunk, chunk), 1)
        causal = col <= row
        for s in range(p_scr.shape[0] // chunk):
            rs = slice(s * chunk, (s + 1) * chunk)
            vb = _rms(v_ref[rs, :], vg_ref[...]).astype(BF16)
            for h in range(groups):
                cs = slice(h * gd, (h + 1) * gd)
                wm = jnp.where(causal, gw_ref[h], 0.0).astype(BF16)
                mixed = _bdot(wm, vb[:, cs]) + gb_ref[:, h:h + 1]
                p_scr[rs, cs] = (u_ref[rs, cs] * mixed).astype(BF16)

    @pl.when(i >= np_tiles)
    def _sample():
        for t in range(steps):
            rs = slice(t * slab, (t + 1) * slab)
            cv_ref[rs, :] = _rms(v_ref[rs, :], vg_ref[...])
        for t in range(steps):
            rs = slice(t * slab, (t + 1) * slab)
            for h in range(groups):
                cs = slice(h * gd, (h + 1) * gd)
                mixed = sw_ref[t * steps:t * steps + 1, cs] * cv_ref[0:slab, cs]
                for s in range(1, t + 1):
                    k = t * steps + s
                    mixed = mixed + sw_ref[k:k + 1, cs] * cv_ref[s * slab:(s + 1) * slab, cs]
                mixed = mixed + sb_ref[t:t + 1, cs]
                p_scr[rs, cs] = (u_ref[rs, cs] * mixed).astype(BF16)

    o_ref[...] = x_ref[...] + _bdot(p_scr[...], w_ref[0])


def _chunk_out(z, x, v_gain, w_s, b_s, w_out, layer, *, m_p, slab, steps):
    m, d = x.shape
    a = z.shape[1] // 2
    groups, chunk = w_s.shape[0], w_s.shape[1]
    gd = a // groups
    assert m_p % TM == 0 and TM % chunk == 0 and m - m_p == TM == slab * steps
    assert steps <= chunk and slab % SUBLANES == 0 and w_out.dtype == BF16
    np_tiles = m_p // TM
    gb = b_s.T
    sw = jnp.repeat(w_s[:, :steps, :steps].transpose(1, 2, 0).reshape(steps * steps, groups), gd, axis=1)
    sb = jnp.repeat(b_s[:, :steps].T, gd, axis=1)
    vm = (4 * TM * a * 4 + TM * a * 2 + 2 * groups * chunk * chunk * 4 + a * d * 2
          + 4 * TM * d * 4 + TM * a * 4 + 4 * (steps * steps + steps) * a * 4 + 2 * TM * d * 4)
    once = pl.Buffered(1)
    kern = functools.partial(_chunk_out_kernel, np_tiles=np_tiles, chunk=chunk, groups=groups,
                             slab=slab, steps=steps)
    return pl.pallas_call(
        kern,
        grid=(m // TM,),
        in_specs=[
            pl.BlockSpec((TM, a), lambda i: (i, 0)),
            pl.BlockSpec((TM, a), lambda i: (i, 1)),
            pl.BlockSpec((1, a), lambda i: (0, 0)),
            pl.BlockSpec((groups, chunk, chunk), lambda i: (0, 0, 0)),
            pl.BlockSpec((chunk, groups), lambda i: (0, 0)),
            pl.BlockSpec((steps * steps, a), lambda i: (0, 0)),
            pl.BlockSpec((steps, a), lambda i: (0, 0)),
            pl.BlockSpec((1, a, d), lambda i: (layer, 0, 0), pipeline_mode=once),
            pl.BlockSpec((TM, d), lambda i: (i, 0)),
        ],
        out_specs=[pl.BlockSpec((TM, d), lambda i: (i, 0)),
                   pl.BlockSpec((TM, a), lambda i: (0, 0), pipeline_mode=once)],
        out_shape=[jax.ShapeDtypeStruct((m, d), F32), jax.ShapeDtypeStruct((TM, a), F32)],
        scratch_shapes=[pltpu.VMEM((TM, a), BF16)],
        compiler_params=pltpu.CompilerParams(
            dimension_semantics=("arbitrary",),
            vmem_limit_bytes=_vmem_limit(vm)),
        name="chunk_out",
    )(z, z, v_gain.reshape(1, a), w_s, gb, sw, sb, w_out, x)


def _conv_out_kernel(bg_ref, z_ref, halo_ref, st_ref, cw_ref, w_ref, x_ref, o_ref, p_scr,
                     *, np_tiles, tiles_per_seq, slab, steps):
    i = pl.program_id(0)
    width = cw_ref.shape[0]
    nhalo = halo_ref.shape[0]
    tm, cdim = p_scr.shape

    @pl.when(i < np_tiles)
    def _prompt():
        keep = (i % tiles_per_seq != 0).astype(F32)
        for c in range(cdim // TN):
            cs = slice(c * TN, (c + 1) * TN)
            ext = jnp.concatenate([halo_ref[:, cs] * keep, z_ref[:, cs]], axis=0)
            conv = cw_ref[0:1, cs] * pltpu.roll(ext, width - 1, axis=0)[nhalo:]
            for k in range(1, width):
                sh = width - 1 - k
                zk = pltpu.roll(ext, sh, axis=0)[nhalo:] if sh else z_ref[:, cs]
                conv = conv + cw_ref[k:k + 1, cs] * zk
            p_scr[:, cs] = (bg_ref[:, cs] * conv).astype(BF16)

    @pl.when(i >= np_tiles)
    def _sample():
        hist = width - 1

        def zrow(r, cs):
            if r < hist:
                return st_ref[r * slab:(r + 1) * slab, cs]
            return z_ref[(r - hist) * slab:(r - hist + 1) * slab, cs]

        for c in range(cdim // TN):
            cs = slice(c * TN, (c + 1) * TN)
            for t in range(steps):
                conv = cw_ref[0:1, cs] * zrow(t, cs)
                for k in range(1, width):
                    conv = conv + cw_ref[k:k + 1, cs] * zrow(t + k, cs)
                rs = slice(t * slab, (t + 1) * slab)
                p_scr[rs, cs] = (bg_ref[rs, cs] * conv).astype(BF16)

    o_ref[...] = x_ref[...] + _bdot(p_scr[...], w_ref[0])


def _conv_out(bg, zc, x, state_t, conv_w, w_out, layer, *, m_p, seq, slab, steps):
    m, d = x.shape
    cd = zc.shape[1]
    width = conv_w.shape[0]
    assert m - m_p == TM == slab * steps and seq % TM == 0 and width - 1 <= SUBLANES
    assert w_out.dtype == BF16
    np_tiles = m_p // TM
    halo_blocks = TM // SUBLANES
    vm = (4 * TM * cd * 4 + 2 * SUBLANES * cd * 4 + (width - 1) * slab * cd * 4 + TM * cd * 2
          + cd * d * 2 + 4 * TM * d * 4 + 2 * TM * d * 4 + 6 * TM * TN * 4)
    kern = functools.partial(_conv_out_kernel, np_tiles=np_tiles, tiles_per_seq=seq // TM,
                             slab=slab, steps=steps)
    once = pl.Buffered(1)
    return pl.pallas_call(
        kern,
        grid=(m // TM,),
        in_specs=[
            pl.BlockSpec((TM, cd), lambda i: (i, 0)),
            pl.BlockSpec((TM, cd), lambda i: (i, 0)),
            pl.BlockSpec((SUBLANES, cd), lambda i: (jnp.maximum(i * halo_blocks - 1, 0), 0)),
            pl.BlockSpec(((width - 1) * slab, cd), lambda i: (0, 0), pipeline_mode=once),
            pl.BlockSpec((width, cd), lambda i: (0, 0)),
            pl.BlockSpec((1, cd, d), lambda i: (layer, 0, 0), pipeline_mode=once),
            pl.BlockSpec((TM, d), lambda i: (i, 0)),
        ],
        out_specs=pl.BlockSpec((TM, d), lambda i: (i, 0)),
        out_shape=jax.ShapeDtypeStruct((m, d), F32),
        scratch_shapes=[pltpu.VMEM((TM, cd), BF16)],
        compiler_params=pltpu.CompilerParams(
            dimension_semantics=("arbitrary",),
            vmem_limit_bytes=_vmem_limit(vm)),
        name="conv_out",
    )(bg, zc, zc, state_t, conv_w, w_out, x)


def _pool_kernel(x_ref, halo_ref, buf_ref, gain_ref, w_ref, sc_ref, o_ref, tail_ref, hs_ref, r_scr,
                 *, np_tiles, tiles_per_seq, slab, steps, windows, pos0):
    i = pl.program_id(0)
    tm = x_ref.shape[0]
    nhalo = halo_ref.shape[0]
    nbuf = buf_ref.shape[0] // slab
    gdim = buf_ref.shape[1]

    @pl.when(i <= np_tiles)
    def _():
        x = x_ref[...]
        r_scr[nhalo:, :] = lax.rsqrt(jnp.mean(x * x, axis=-1, keepdims=True) + EPS)
        hx = halo_ref[...]
        r_scr[:nhalo, :] = lax.rsqrt(jnp.mean(hx * hx, axis=-1, keepdims=True) + EPS)

    def normed(cs):
        return x_ref[:, cs] * r_scr[nhalo:, :] * gain_ref[:, cs]

    def finish(gi, cs, h, pooled):
        diff = (pooled - h).astype(BF16)
        y = _bdot(diff, w_ref[gi].astype(BF16)) * sc_ref[:, cs]
        o_ref[:, cs] = x_ref[:, cs] + y

    @pl.when(i < np_tiles)
    def _prompt():
        seq_tile = i % tiles_per_seq
        keep = (seq_tile != 0).astype(F32)
        pos = seq_tile * tm + lax.broadcasted_iota(jnp.int32, (tm, 1), 0)
        for gi, win in enumerate(windows):
            cs = slice(gi * gdim, (gi + 1) * gdim)
            h = normed(cs)
            tail_ref[:, cs] = h[tm - nhalo:]
            hh = halo_ref[:, cs] * r_scr[:nhalo, :] * gain_ref[:, cs] * keep
            s = jnp.concatenate([hh, h], axis=0)
            k = 1
            while k < win:
                s = s + pltpu.roll(s, k, axis=0)
                k *= 2
            count = jnp.minimum(win, pos + 1).astype(F32)
            finish(gi, cs, h, s[nhalo:] / count)

    for gi, win in enumerate(windows):
        @pl.when(i == np_tiles + gi)
        def _sample(gi=gi, win=win):
            cs = slice(gi * gdim, (gi + 1) * gdim)
            h = normed(cs)
            hs_ref[:, cs] = h

            def hrow(r):
                if r < nbuf:
                    return buf_ref[r * slab:(r + 1) * slab, :]
                return h[(r - nbuf) * slab:(r - nbuf + 1) * slab]

            rows = []
            for t in range(steps):
                acc = hrow(nbuf + t - win + 1)
                for r in range(nbuf + t - win + 2, nbuf + t + 1):
                    acc = acc + hrow(r)
                rows.append(acc / float(min(win, pos0 + t + 1)))
            finish(gi, cs, h, jnp.concatenate(rows, axis=0))


def _pool_mixer(x, gain, buf_t, w_group, scale, *, m_p, seq, slab, steps):
    m, d = x.shape
    ng, gdim = w_group.shape[0], w_group.shape[1]
    nhalo = 2 * SUBLANES
    assert ng == len(POOL_WINDOWS) and max(POOL_WINDOWS) <= nhalo and gdim * ng == d
    assert m - m_p == TM == slab * steps and seq % TM == 0
    assert buf_t.shape[0] // slab >= max(POOL_WINDOWS) - 1
    np_tiles = m_p // TM
    halo_blocks = TM // nhalo
    nbuf_rows = buf_t.shape[0]
    vm = (4 * TM * d * 4 + 2 * nhalo * d * 4 + 2 * nbuf_rows * gdim * 4 + ng * gdim * gdim * 4
          + gdim * gdim * 2 + TM * d * 4 + 2 * nhalo * d * 4 + (TM + nhalo) * LANES * 4 + 12 * TM * gdim * 4)
    kern = functools.partial(_pool_kernel, np_tiles=np_tiles, tiles_per_seq=seq // TM, slab=slab,
                             steps=steps, windows=POOL_WINDOWS, pos0=PAST_LEN)
    tile = lambda i: jnp.minimum(i, np_tiles)
    once = pl.Buffered(1)
    return pl.pallas_call(
        kern,
        grid=(np_tiles + ng,),
        in_specs=[
            pl.BlockSpec((TM, d), lambda i: (tile(i), 0)),
            pl.BlockSpec((nhalo, d), lambda i: (jnp.maximum(tile(i) * halo_blocks - 1, 0), 0)),
            pl.BlockSpec((nbuf_rows, gdim), lambda i: (0, jnp.maximum(i - np_tiles, 0))),
            pl.BlockSpec((1, d), lambda i: (0, 0)),
            pl.BlockSpec((ng, gdim, gdim), lambda i: (0, 0, 0), pipeline_mode=once),
            pl.BlockSpec((1, d), lambda i: (0, 0)),
        ],
        out_specs=[pl.BlockSpec((TM, d), lambda i: (tile(i), 0)),
                   pl.BlockSpec((nhalo, d), lambda i: (jnp.minimum(i, np_tiles - 1), 0)),
                   pl.BlockSpec((TM, d), lambda i: (0, 0), pipeline_mode=once)],
        out_shape=[jax.ShapeDtypeStruct((m, d), F32), jax.ShapeDtypeStruct((np_tiles * nhalo, d), F32),
                   jax.ShapeDtypeStruct((TM, d), F32)],
        scratch_shapes=[pltpu.VMEM((TM + nhalo, 1), F32)],
        compiler_params=pltpu.CompilerParams(
            dimension_semantics=("arbitrary",),
            vmem_limit_bytes=_vmem_limit(vm)),
        name="pool_mixer",
    )(x, x, buf_t, gain.reshape(1, d), w_group, scale.reshape(1, d))


def _ffn_kernel(ce_ref, nu_ref, nv_ref, xb_ref, x_ref, gain_ref, wg_hbm, wu_hbm, wd_hbm, o_ref,
                wg_buf, wu_buf, wd_buf, sems, maybe_xb_scr=None, *, dense, nf):
    del xb_ref
    c = pl.program_id(0)
    nch = pl.num_programs(0)
    rows = x_ref.shape[0]
    nvalid = nv_ref[c]
    units = nu_ref[c]

    def weight_copies(chunk, f, slot):
        e = ce_ref[chunk]
        cols = pl.ds(pl.multiple_of(f * FFN_TF, FFN_TF), FFN_TF)
        return (pltpu.make_async_copy(wg_hbm.at[e, :, cols], wg_buf.at[slot], sems.at[0, slot]),
                pltpu.make_async_copy(wu_hbm.at[e, :, cols], wu_buf.at[slot], sems.at[1, slot]),
                pltpu.make_async_copy(wd_hbm.at[e, cols, :], wd_buf.at[slot], sems.at[2, slot]))

    def start_weights(chunk, f, slot):
        for copy in weight_copies(chunk, f, slot):
            copy.start()

    def wait_weights(chunk, f, slot):
        for copy in weight_copies(chunk, f, slot):
            copy.wait()

    @pl.when(jnp.logical_and(c == 0, units > 0))
    def _prime():
        start_weights(0, 0, 0)

    for s in range(rows // FFN_UNIT):
        rs = slice(s * FFN_UNIT, (s + 1) * FFN_UNIT)
        if dense:
            x = x_ref[rs, :]
            live = (s * FFN_UNIT + lax.broadcasted_iota(jnp.int32, (FFN_UNIT, 1), 0)) < nvalid
            o_ref[rs, :] = x
            maybe_xb_scr[rs, :] = jnp.where(live, _rms(x, gain_ref[...]), 0.0).astype(BF16)
        else:
            o_ref[rs, :] = jnp.zeros((FFN_UNIT, o_ref.shape[1]), F32)

    def sub_tile(slot, start, nrows):
        rs = pl.ds(pl.multiple_of(start, FFN_UNIT), nrows)
        xs = maybe_xb_scr[rs, :] if dense else x_ref[rs, :].astype(BF16)
        gate = _bdot(xs, wg_buf[slot].astype(BF16))
        up = _bdot(xs, wu_buf[slot].astype(BF16))
        act = (gate * jax.nn.sigmoid(gate) * up).astype(BF16)
        o_ref[rs, :] += _bdot(act, wd_buf[slot].astype(BF16))

    def swiglu_block(slot):
        full = rows // FFN_UNIT

        @pl.when(units == full)
        def _full():
            sub_tile(slot, 0, rows)

        @pl.when(units < full)
        def _partial():
            nquad = lax.shift_right_logical(units, 2)

            def quad(s, carry):
                sub_tile(slot, s * (4 * FFN_UNIT), 4 * FFN_UNIT)
                return carry

            lax.fori_loop(0, nquad, quad, 0)
            has_pair = lax.bitwise_and(units, 2)

            @pl.when(has_pair != 0)
            def _pair():
                sub_tile(slot, nquad * (4 * FFN_UNIT), 2 * FFN_UNIT)

            @pl.when(lax.bitwise_and(units, 1) != 0)
            def _single():
                sub_tile(slot, nquad * (4 * FFN_UNIT) + has_pair * FFN_UNIT, FFN_UNIT)

    @pl.when(units > 0)
    def _work():
        nxt = jnp.minimum(c + 1, nch - 1)
        next_live = jnp.logical_and(c + 1 < nch, nu_ref[nxt] > 0)

        def block(f, carry):
            slot = lax.bitwise_and(f, 1)
            wait_weights(c, f, slot)

            @pl.when(f + 1 < nf)
            def _():
                start_weights(c, f + 1, 1 - slot)

            @pl.when(jnp.logical_and(f + 1 == nf, next_live))
            def _():
                start_weights(nxt, 0, 1 - slot)

            swiglu_block(slot)
            return carry

        lax.fori_loop(0, nf, block, 0)


def _ffn(x, gain, wg, wu, wd, chunk_expert, chunk_units, chunk_nvalid, *, dense):
    m, d = x.shape
    dff = wg.shape[-1]
    nch = chunk_expert.shape[0]
    nf = dff // FFN_TF
    assert dff % FFN_TF == 0 and nf % 2 == 0
    last_live = jnp.maximum(jnp.sum(chunk_units > 0) - 1, 0).astype(jnp.int32)
    x_block = jnp.minimum(jnp.arange(nch, dtype=jnp.int32), last_live)
    vm = (2 * FFN_ROWS * d * 4 + FFN_ROWS * d * 2 + 3 * 2 * d * FFN_TF * 4
          + 3 * d * FFN_TF * 2 + 16 * FFN_UNIT * FFN_TF * 4)
    once = pl.Buffered(1)
    grid_spec = pltpu.PrefetchScalarGridSpec(
        num_scalar_prefetch=4,
        grid=(nch,),
        in_specs=[
            pl.BlockSpec((FFN_ROWS, d), lambda c, ce, nu, nv, xb: (xb[c], 0), pipeline_mode=once),
            pl.BlockSpec((1, d), lambda c, ce, nu, nv, xb: (0, 0)),
            pl.BlockSpec(memory_space=pl.ANY),
            pl.BlockSpec(memory_space=pl.ANY),
            pl.BlockSpec(memory_space=pl.ANY),
        ],
        out_specs=pl.BlockSpec((FFN_ROWS, d), lambda c, ce, nu, nv, xb: (c, 0),
                               pipeline_mode=once if dense else None),
        scratch_shapes=[pltpu.VMEM((2, d, FFN_TF), F32), pltpu.VMEM((2, d, FFN_TF), F32),
                        pltpu.VMEM((2, FFN_TF, d), F32), pltpu.SemaphoreType.DMA((3, 2))]
        + ([pltpu.VMEM((FFN_ROWS, d), BF16)] if dense else []),
    )
    return pl.pallas_call(
        functools.partial(_ffn_kernel, dense=dense, nf=nf),
        grid_spec=grid_spec,
        out_shape=jax.ShapeDtypeStruct((m, d), F32),
        compiler_params=pltpu.CompilerParams(
            dimension_semantics=("arbitrary",),
            vmem_limit_bytes=_vmem_limit(vm)),
        name="ffn_dense" if dense else "ffn_experts",
    )(chunk_expert, chunk_units, chunk_nvalid, x_block, x, gain.reshape(1, d), wg, wu, wd)


def _dense_ffn(x, gain, wg, wu, wd, layer):
    m = x.shape[0]
    nch = pl.cdiv(m, FFN_ROWS)
    nvalid = jnp.minimum(FFN_ROWS, m - FFN_ROWS * jnp.arange(nch, dtype=jnp.int32))
    units = (nvalid + FFN_UNIT - 1) // FFN_UNIT
    which = jnp.full((nch,), layer, jnp.int32)
    return _ffn(x, gain, wg, wu, wd, which, units, nvalid, dense=True)


def _route_kernel(x_ref, g_ref, r_ref, hn_ref, info_ref, cnt_ref, carry_scr, *, n_experts):
    i = pl.program_id(0)
    tm = x_ref.shape[0]

    @pl.when(i == 0)
    def _():
        carry_scr[...] = jnp.zeros_like(carry_scr)

    hn = _rms(x_ref[...], g_ref[...])
    hn_ref[...] = hn
    lane = lax.broadcasted_iota(jnp.int32, (tm, LANES), 1).astype(F32)
    logits = jnp.where(lane < n_experts, _split_dot(hn, r_ref[...]), -jnp.inf)
    m1 = jnp.max(logits, axis=-1, keepdims=True)
    i1 = jnp.min(jnp.where(logits == m1, lane, float(LANES)), axis=-1, keepdims=True)
    rest = jnp.where(lane == i1, -jnp.inf, logits)
    m2 = jnp.max(rest, axis=-1, keepdims=True)
    i2 = jnp.min(jnp.where(rest == m2, lane, float(LANES)), axis=-1, keepdims=True)
    e = jnp.exp(m2 - m1)
    g1 = 1.0 / (1.0 + e)
    g2 = e / (1.0 + e)
    oh1 = (lane == i1).astype(F32)
    oh2 = (lane == i2).astype(F32)
    cnt = oh1 + oh2
    row = lax.broadcasted_iota(jnp.int32, (tm, tm), 0)
    col = lax.broadcasted_iota(jnp.int32, (tm, tm), 1)
    before = (col < row).astype(BF16)
    ranks = _bdot(before, cnt.astype(BF16)) + carry_scr[0:1, :]
    rank1 = jnp.sum(ranks * oh1, axis=-1, keepdims=True)
    rank2 = jnp.sum(ranks * oh2, axis=-1, keepdims=True)
    carry_scr[0:1, :] = carry_scr[0:1, :] + jnp.sum(cnt, axis=0, keepdims=True)
    cnt_ref[...] = jnp.broadcast_to(carry_scr[0:1, :], cnt_ref.shape)
    info = jnp.where(lane == 0, i1, 0.0)
    info = jnp.where(lane == 1, i2, info)
    info = jnp.where(lane == 2, g1, info)
    info = jnp.where(lane == 3, g2, info)
    info = jnp.where(lane == 4, rank1, info)
    info = jnp.where(lane == 5, rank2, info)
    info_ref[...] = info


def _route(x, gain, router):
    m, d = x.shape
    ne = router.shape[1]
    rpad = jnp.pad(router, ((0, 0), (0, LANES - ne)))
    vm = 4 * TM * d * 4 + 2 * d * LANES * 4 + 4 * TM * LANES * 4 + TM * TM * 8 + 8 * TM * d * 4
    return pl.pallas_call(
        functools.partial(_route_kernel, n_experts=ne),
        grid=(m // TM,),
        in_specs=[
            pl.BlockSpec((TM, d), lambda i: (i, 0)),
            pl.BlockSpec((1, d), lambda i: (0, 0)),
            pl.BlockSpec((d, LANES), lambda i: (0, 0)),
        ],
        out_specs=[pl.BlockSpec((TM, d), lambda i: (i, 0)),
                   pl.BlockSpec((TM, LANES), lambda i: (i, 0)),
                   pl.BlockSpec((SUBLANES, LANES), lambda i: (0, 0))],
        out_shape=[jax.ShapeDtypeStruct((m, d), F32), jax.ShapeDtypeStruct((m, LANES), F32),
                   jax.ShapeDtypeStruct((SUBLANES, LANES), F32)],
        scratch_shapes=[pltpu.VMEM((SUBLANES, LANES), F32)],
        compiler_params=pltpu.CompilerParams(
            dimension_semantics=("arbitrary",), vmem_limit_bytes=_vmem_limit(vm)),
        name="route",
    )(x, gain.reshape(1, d), rpad)


def _row_copy(src, src_row, dst, dst_row, sem):
    return pltpu.make_async_copy(src.at[pl.ds(src_row, 1), :], dst.at[pl.ds(dst_row, 1), :], sem)


def _dispatch_kernel(p1_ref, p2_ref, hn_ref, init_ref, xs_ref, sem):
    del init_ref
    base = pl.program_id(0) * hn_ref.shape[0]
    tm = hn_ref.shape[0]

    def start(r, carry):
        _row_copy(hn_ref, r, xs_ref, p1_ref[base + r], sem).start()
        _row_copy(hn_ref, r, xs_ref, p2_ref[base + r], sem).start()
        return carry

    def wait(r, carry):
        _row_copy(hn_ref, r, xs_ref, p1_ref[base + r], sem).wait()
        _row_copy(hn_ref, r, xs_ref, p2_ref[base + r], sem).wait()
        return carry

    lax.fori_loop(0, tm, start, 0, unroll=ROW_DMA_UNROLL)
    lax.fori_loop(0, tm, wait, 0, unroll=ROW_DMA_UNROLL)


def _dispatch(hn, pos1, pos2, init):
    m, d = hn.shape
    n_rows = init.shape[0]
    assert init.shape == (n_rows, d) and init.dtype == F32
    grid_spec = pltpu.PrefetchScalarGridSpec(
        num_scalar_prefetch=2,
        grid=(m // ROUTE_TM,),
        in_specs=[pl.BlockSpec((ROUTE_TM, d), lambda i, p1, p2: (i, 0)),
                  pl.BlockSpec(memory_space=pl.ANY)],
        out_specs=pl.BlockSpec(memory_space=pl.ANY),
        scratch_shapes=[pltpu.SemaphoreType.DMA(())],
    )
    return pl.pallas_call(
        _dispatch_kernel,
        grid_spec=grid_spec,
        out_shape=jax.ShapeDtypeStruct((n_rows, d), F32),
        input_output_aliases={3: 0},
        compiler_params=pltpu.CompilerParams(dimension_semantics=("arbitrary",)),
        name="dispatch",
    )(pos1, pos2, hn, init)


def _combine_kernel(p1_ref, p2_ref, x_ref, info_ref, gain_ref, ys_ref, *refs, final_norm, head_tiles):
    *out_refs, b1_scr, b2_scr, sems = refs
    tm = x_ref.shape[0]
    i = pl.program_id(0)
    slot = lax.bitwise_and(i, 1)

    def gather(tile, sl, go):
        def row(r, carry):
            t = tile * tm + r
            for pos_ref, buf in ((p1_ref, b1_scr), (p2_ref, b2_scr)):
                copy = _row_copy(ys_ref, pos_ref[t], buf.at[sl], r, sems.at[sl])
                copy.start() if go else copy.wait()
            return carry

        lax.fori_loop(0, tm, row, 0, unroll=ROW_DMA_UNROLL)

    @pl.when(i == 0)
    def _():
        gather(0, 0, True)

    @pl.when(i + 1 < pl.num_programs(0))
    def _():
        gather(i + 1, 1 - slot, True)

    gather(i, slot, False)
    info = info_ref[...]
    e1, e2 = info[:, 0:1], info[:, 1:2]
    g1, g2 = info[:, 2:3], info[:, 3:4]
    y1 = g1 * b1_scr[slot]
    y2 = g2 * b2_scr[slot]
    lo = jnp.where(e1 < e2, y1, y2)
    hi = jnp.where(e1 < e2, y2, y1)
    out = x_ref[...] + (lo + hi)
    if final_norm:
        out = _rms(out, gain_ref[...])
    if head_tiles is None:
        out_refs[0][...] = out
    else:
        head_ref, tail_ref = out_refs

        @pl.when(i < head_tiles)
        def _():
            head_ref[...] = out

        @pl.when(i >= head_tiles)
        def _():
            tail_ref[...] = out


def _combine(x, info, ys, pos1, pos2, gain, *, final_norm, head_rows=None):
    m, d = x.shape
    vm = 6 * ROUTE_TM * d * 4 + 2 * ROUTE_TM * LANES * 4 + 4 * ROUTE_TM * d * 4 + 6 * ROUTE_TM * d * 4
    if head_rows is None:
        head_tiles = None
        out_specs = pl.BlockSpec((ROUTE_TM, d), lambda i, p1, p2: (i, 0))
        out_shape = jax.ShapeDtypeStruct((m, d), F32)
    else:
        assert head_rows % ROUTE_TM == 0 and 0 < head_rows < m
        head_tiles = head_rows // ROUTE_TM
        out_specs = [pl.BlockSpec((ROUTE_TM, d), lambda i, p1, p2: (jnp.minimum(i, head_tiles - 1), 0)),
                     pl.BlockSpec((ROUTE_TM, d), lambda i, p1, p2: (jnp.maximum(i - head_tiles, 0), 0))]
        out_shape = [jax.ShapeDtypeStruct((head_rows, d), F32), jax.ShapeDtypeStruct((m - head_rows, d), F32)]
    grid_spec = pltpu.PrefetchScalarGridSpec(
        num_scalar_prefetch=2,
        grid=(m // ROUTE_TM,),
        in_specs=[
            pl.BlockSpec((ROUTE_TM, d), lambda i, p1, p2: (i, 0)),
            pl.BlockSpec((ROUTE_TM, LANES), lambda i, p1, p2: (i, 0)),
            pl.BlockSpec((1, d), lambda i, p1, p2: (0, 0)),
            pl.BlockSpec(memory_space=pl.ANY),
        ],
        out_specs=out_specs,
        scratch_shapes=[pltpu.VMEM((2, ROUTE_TM, d), F32), pltpu.VMEM((2, ROUTE_TM, d), F32),
                        pltpu.SemaphoreType.DMA((2,))],
    )
    return pl.pallas_call(
        functools.partial(_combine_kernel, final_norm=final_norm, head_tiles=head_tiles),
        grid_spec=grid_spec,
        out_shape=out_shape,
        compiler_params=pltpu.CompilerParams(
            dimension_semantics=("arbitrary",), vmem_limit_bytes=_vmem_limit(vm)),
        name="combine_final" if final_norm else "combine",
    )(pos1, pos2, x, info, gain.reshape(1, d), ys)


def _moe_ffn(x, gain, router, wg, wu, wd, layer, out_gain, *, final_norm, head_rows=None, spare=None):
    m, d = x.shape
    ne = router.shape[1]
    wg, wu, wd = (w.reshape((-1,) + w.shape[2:]) for w in (wg, wu, wd))
    assert m % TM == 0 and m % ROUTE_TM == 0
    hn, info, cnt = _route(x, gain, router)
    counts = cnt[0, :ne].astype(jnp.int32)
    chunks_per = (counts + FFN_ROWS - 1) // FFN_ROWS
    chunk_end = jnp.cumsum(chunks_per)
    chunk_start = chunk_end - chunks_per
    share = (counts + jnp.maximum(chunks_per, 1) - 1) // jnp.maximum(chunks_per, 1)
    share = jnp.maximum((share + FFN_UNIT - 1) // FFN_UNIT * FFN_UNIT, FFN_UNIT)

    def position(e, rank):
        j = rank // share[e]
        return (chunk_start[e] + j) * FFN_ROWS + rank - j * share[e]

    pos1 = position(info[:, 0].astype(jnp.int32), info[:, 4].astype(jnp.int32))
    pos2 = position(info[:, 1].astype(jnp.int32), info[:, 5].astype(jnp.int32))
    nch = (m * TOP_K) // FFN_ROWS + ne
    cidx = jnp.arange(nch, dtype=jnp.int32)
    last = jnp.maximum(chunk_end[-1] - 1, 0)
    owner = jnp.sum(jnp.minimum(cidx, last)[:, None] >= chunk_end[None, :], axis=1).astype(jnp.int32)
    owner = jnp.minimum(owner, ne - 1)
    local = cidx - chunk_start[owner]
    nvalid = jnp.clip(counts[owner] - local * share[owner], 0, share[owner])
    nvalid = jnp.where(cidx < chunk_end[-1], nvalid, 0).astype(jnp.int32)
    units = (nvalid + FFN_UNIT - 1) // FFN_UNIT
    if spare is None:
        spare = jnp.zeros((nch * FFN_ROWS, d), F32)
    xs = _dispatch(hn, pos1, pos2, spare)
    ys = _ffn(xs, gain, wg, wu, wd, owner + layer * ne, units, nvalid, dense=False)
    out = _combine(x, info, ys, pos1, pos2, out_gain, final_norm=final_norm, head_rows=head_rows)
    return out, ys


def kernel(x_prompt, x_sample, state_conv, state_pool, norm_mix, norm_ffn, final_norm, a_w_in, a_v_gain, a_w_s, a_b_s, a_w_out, b_w_in, b_conv, b_w_out, c_w_group, c_scale, ffn_w_gate, ffn_w_up, ffn_w_down, moe_router, moe_w_gate, moe_w_up, moe_w_down):
    batch, seq, d = x_prompt.shape
    slab, steps, _ = x_sample.shape
    depth = norm_mix.shape[0]
    assert depth % 2 == 0, "the final RMSNorm is fused into the routed combine kernel of the last layer"
    m_p = batch * seq
    geo = dict(m_p=m_p, slab=slab, steps=steps)

    def to_time_major(a):
        return a.transpose(1, 0, 2).reshape(a.shape[1] * slab, a.shape[2])

    def from_time_major(a, r):
        return a.reshape(r, slab, a.shape[-1]).transpose(1, 0, 2)

    def seq_tails(a, r):
        return jnp.stack([a[(b + 1) * seq - r:(b + 1) * seq] for b in range(batch)])

    a_w_out_b = a_w_out.astype(BF16)
    b_w_out_b = b_w_out.astype(BF16)
    x = jnp.concatenate([x_prompt.reshape(m_p, d), to_time_major(x_sample)], axis=0)
    chunk_v_s, conv_p, conv_s, pool_p, pool_s = [], [], [], [], []
    spare = None
    for i in range(depth):
        j, kind = divmod(i, 3)
        if kind == 0:
            z = _inproj_gelu(x, norm_mix[i], a_w_in, j)
            x, cv = _chunk_out(z, x, a_v_gain[j], a_w_s[j], a_b_s[j], a_w_out_b, j, **geo)
            chunk_v_s.append(from_time_major(cv, steps))
        elif kind == 1:
            hist = b_conv.shape[1] - 1
            bg, zc = _inproj_conv(x, norm_mix[i], b_w_in, j)
            x = _conv_out(bg, zc, x, to_time_major(state_conv[j]), b_conv[j], b_w_out_b, j, seq=seq, **geo)
            conv_p.append(seq_tails(zc, hist))
            conv_s.append(from_time_major(zc[m_p:], steps)[:, steps - hist:])
        else:
            nbuf = state_pool.shape[2]
            x, tails, hn_s = _pool_mixer(x, norm_mix[i], to_time_major(state_pool[j]), c_w_group[j],
                                         c_scale[j], seq=seq, **geo)
            per_tile = tails.shape[0] // (m_p // TM)
            ends = tails.reshape(batch, seq // TM, per_tile, d)[:, -1]
            pool_p.append(ends[:, per_tile - nbuf:])
            hc = jnp.concatenate([state_pool[j], from_time_major(hn_s, steps)], axis=1)
            pool_s.append(hc[:, hc.shape[1] - nbuf:])
        f = i // 2
        if i % 2 == 0:
            x = _dense_ffn(x, norm_ffn[i], ffn_w_gate, ffn_w_up, ffn_w_down, f)
        else:
            last = i == depth - 1
            x, spare = _moe_ffn(x, norm_ffn[i], moe_router[f], moe_w_gate, moe_w_up, moe_w_down, f,
                                final_norm, final_norm=last, head_rows=m_p if last else None, spare=spare)
    y_prompt = x[0].reshape(batch, seq, d)
    y_sample = from_time_major(x[1], steps)
    return (y_prompt, y_sample, jnp.stack(chunk_v_s), jnp.stack(conv_p), jnp.stack(conv_s),
            jnp.stack(pool_p), jnp.stack(pool_s))
```

```python
import functools
import math

import jax
import jax.numpy as jnp
from jax import lax
from jax.experimental import pallas as pl
from jax.experimental.pallas import tpu as pltpu

F32 = jnp.float32
BF16 = jnp.bfloat16
EPS = 1e-6
PAST_LEN = 16384
POOL_WINDOWS = (2, 4, 8, 16)
TOP_K = 2

V7X_VMEM_BYTES = 64 * 1024 * 1024
SUBLANES = 8
LANES = 128

TM = 512
TN = 512
FFN_UNIT = 64
FFN_ROWS = 18 * FFN_UNIT
FFN_STRAIGHT = 3
FFN_TF = 512
ROUTE_TM = 256
ROW_DMA_UNROLL = 32


def _vmem_limit(nbytes):
    return min(int(nbytes) + (6 << 20), V7X_VMEM_BYTES - (4 << 20))


def _rms(x, gain):
    ms = jnp.mean(x * x, axis=-1, keepdims=True)
    return x * lax.rsqrt(ms + EPS) * gain


def _gelu_tanh(x):
    c = math.sqrt(2.0 / math.pi)
    return 0.5 * x * (1.0 + jnp.tanh(c * (x + 0.044715 * (x * x * x))))


def _bdot(a, b):
    return jnp.dot(a, b, preferred_element_type=F32)


def _split_dot(a, b):
    ah = a.astype(BF16)
    al = (a - ah.astype(F32)).astype(BF16)
    bh = b.astype(BF16)
    bl = (b - bh.astype(F32)).astype(BF16)
    return _bdot(ah, bh) + _bdot(ah, bl) + _bdot(al, bh)


INPROJ_ROW_TILES = 8


def _inproj_rows(m):
    assert m % (INPROJ_ROW_TILES * 2 * SUBLANES) == 0
    return m // INPROJ_ROW_TILES


def _inproj_gelu_kernel(x_ref, g_ref, w_ref, o_ref, h_scr):
    @pl.when(pl.program_id(1) == 0)
    def _():
        h_scr[...] = _rms(x_ref[...], g_ref[...]).astype(BF16)

    z = _bdot(h_scr[...], w_ref[0].astype(BF16))
    o_ref[...] = _gelu_tanh(z)


def _inproj_gelu(x, gain, w, layer):
    m, d = x.shape
    n = w.shape[2]
    tm = _inproj_rows(m)
    tn = 2 * TN
    vm = 2 * tm * d * 4 + tm * d * 2 + 2 * d * tn * 4 + d * tn * 2 + 4 * tm * tn * 4
    return pl.pallas_call(
        _inproj_gelu_kernel,
        grid=(m // tm, n // tn),
        in_specs=[
            pl.BlockSpec((tm, d), lambda i, j: (i, 0)),
            pl.BlockSpec((1, d), lambda i, j: (0, 0)),
            pl.BlockSpec((1, d, tn), lambda i, j: (layer, 0, j)),
        ],
        out_specs=pl.BlockSpec((tm, tn), lambda i, j: (i, j)),
        out_shape=jax.ShapeDtypeStruct((m, n), F32),
        scratch_shapes=[pltpu.VMEM((tm, d), BF16)],
        compiler_params=pltpu.CompilerParams(
            dimension_semantics=("arbitrary", "arbitrary"),
            vmem_limit_bytes=_vmem_limit(vm)),
        name="inproj_gelu",
    )(x, gain.reshape(1, d), w)


def _inproj_conv_kernel(x_ref, g_ref, wb_ref, wc_ref, wh_ref, ob_ref, oz_ref, h_scr):
    @pl.when(pl.program_id(1) == 0)
    def _():
        h_scr[...] = _rms(x_ref[...], g_ref[...]).astype(BF16)

    h = h_scr[...]
    ob_ref[...] = _bdot(h, wb_ref[0].astype(BF16))
    c = _bdot(h, wc_ref[0].astype(BF16))
    hin = _bdot(h, wh_ref[0].astype(BF16))
    oz_ref[...] = c * hin


def _inproj_conv(x, gain, w, layer):
    m, d = x.shape
    cd = w.shape[2] // 3
    tm = _inproj_rows(m)
    tn = TN // 2
    nb = cd // tn
    vm = 2 * tm * d * 4 + tm * d * 2 + 3 * (2 * d * tn * 4 + d * tn * 2) + 10 * tm * tn * 4
    out = jax.ShapeDtypeStruct((m, cd), F32)
    return pl.pallas_call(
        _inproj_conv_kernel,
        grid=(m // tm, nb),
        in_specs=[
            pl.BlockSpec((tm, d), lambda i, j: (i, 0)),
            pl.BlockSpec((1, d), lambda i, j: (0, 0)),
            pl.BlockSpec((1, d, tn), lambda i, j: (layer, 0, j)),
            pl.BlockSpec((1, d, tn), lambda i, j: (layer, 0, j + nb)),
            pl.BlockSpec((1, d, tn), lambda i, j: (layer, 0, j + 2 * nb)),
        ],
        out_specs=[pl.BlockSpec((tm, tn), lambda i, j: (i, j)),
                   pl.BlockSpec((tm, tn), lambda i, j: (i, j))],
        out_shape=[out, out],
        scratch_shapes=[pltpu.VMEM((tm, d), BF16)],
        compiler_params=pltpu.CompilerParams(
            dimension_semantics=("arbitrary", "arbitrary"),
            vmem_limit_bytes=_vmem_limit(vm)),
        name="inproj_conv",
    )(x, gain.reshape(1, d), w, w, w)


def _chunk_out_kernel(u_ref, v_ref, vg_ref, gw_ref, gb_ref, sw_ref, sb_ref, w_ref, x_ref,
                      o_ref, cv_ref, p_scr, *, np_tiles, chunk, groups, slab, steps):
    i = pl.program_id(0)
    gd = p_scr.shape[1] // groups

    @pl.when(i < np_tiles)
    def _prompt():
        row = lax.broadcasted_iota(jnp.int32, (chunk, chunk), 0)
        col = lax.broadcasted_iota(jnp.int32, (chunk, chunk), 1)
        causal = col <= row
        for s in range(p_scr.shape[0] // chunk):
            rs = slice(s * chunk, (s + 1) * chunk)
            vb = _rms(v_ref[rs, :], vg_ref[...]).astype(BF16)
            for h in range(groups):
                cs = slice(h * gd, (h + 1) * gd)
                wm = jnp.where(causal, gw_ref[h], 0.0).astype(BF16)
                mixed = _bdot(wm, vb[:, cs]) + gb_ref[:, h:h + 1]
                p_scr[rs, cs] = (u_ref[rs, cs] * mixed).astype(BF16)

    @pl.when(i >= np_tiles)
    def _sample():
        for t in range(steps):
            rs = slice(t * slab, (t + 1) * slab)
            cv_ref[rs, :] = _rms(v_ref[rs, :], vg_ref[...])
        for t in range(steps):
            rs = slice(t * slab, (t + 1) * slab)
            for h in range(groups):
                cs = slice(h * gd, (h + 1) * gd)
                mixed = sw_ref[t * steps:t * steps + 1, cs] * cv_ref[0:slab, cs]
                for s in range(1, t + 1):
                    k = t * steps + s
                    mixed = mixed + sw_ref[k:k + 1, cs] * cv_ref[s * slab:(s + 1) * slab, cs]
                mixed = mixed + sb_ref[t:t + 1, cs]
                p_scr[rs, cs] = (u_ref[rs, cs] * mixed).astype(BF16)

    o_ref[...] = x_ref[...] + _bdot(p_scr[...], w_ref[0])


def _chunk_out(z, x, v_gain, w_s, b_s, w_out, layer, *, m_p, slab, steps):
    m, d = x.shape
    a = z.shape[1] // 2
    groups, chunk = w_s.shape[0], w_s.shape[1]
    gd = a // groups
    assert m_p % TM == 0 and TM % chunk == 0 and m - m_p == TM == slab * steps
    assert steps <= chunk and slab % SUBLANES == 0 and w_out.dtype == BF16
    np_tiles = m_p // TM
    gb = b_s.T
    sw = jnp.repeat(w_s[:, :steps, :steps].transpose(1, 2, 0).reshape(steps * steps, groups), gd, axis=1)
    sb = jnp.repeat(b_s[:, :steps].T, gd, axis=1)
    vm = (4 * TM * a * 4 + TM * a * 2 + 2 * groups * chunk * chunk * 4 + a * d * 2
          + 4 * TM * d * 4 + TM * a * 4 + 4 * (steps * steps + steps) * a * 4 + 2 * TM * d * 4)
    once = pl.Buffered(1)
    kern = functools.partial(_chunk_out_kernel, np_tiles=np_tiles, chunk=chunk, groups=groups,
                             slab=slab, steps=steps)
    return pl.pallas_call(
        kern,
        grid=(m // TM,),
        in_specs=[
            pl.BlockSpec((TM, a), lambda i: (i, 0)),
            pl.BlockSpec((TM, a), lambda i: (i, 1)),
            pl.BlockSpec((1, a), lambda i: (0, 0)),
            pl.BlockSpec((groups, chunk, chunk), lambda i: (0, 0, 0)),
            pl.BlockSpec((chunk, groups), lambda i: (0, 0)),
            pl.BlockSpec((steps * steps, a), lambda i: (0, 0)),
            pl.BlockSpec((steps, a), lambda i: (0, 0)),
            pl.BlockSpec((1, a, d), lambda i: (layer, 0, 0), pipeline_mode=once),
            pl.BlockSpec((TM, d), lambda i: (i, 0)),
        ],
        out_specs=[pl.BlockSpec((TM, d), lambda i: (i, 0)),
                   pl.BlockSpec((TM, a), lambda i: (0, 0), pipeline_mode=once)],
        out_shape=[jax.ShapeDtypeStruct((m, d), F32), jax.ShapeDtypeStruct((TM, a), F32)],
        scratch_shapes=[pltpu.VMEM((TM, a), BF16)],
        compiler_params=pltpu.CompilerParams(
            dimension_semantics=("arbitrary",),
            vmem_limit_bytes=_vmem_limit(vm)),
        name="chunk_out",
    )(z, z, v_gain.reshape(1, a), w_s, gb, sw, sb, w_out, x)


def _conv_out_kernel(bg_ref, z_ref, halo_ref, st_ref, cw_ref, w_ref, x_ref, o_ref, p_scr,
                     *, np_tiles, tiles_per_seq, slab, steps):
    i = pl.program_id(0)
    width = cw_ref.shape[0]
    nhalo = halo_ref.shape[0]
    tm, cdim = p_scr.shape

    @pl.when(i < np_tiles)
    def _prompt():
        keep = (i % tiles_per_seq != 0).astype(F32)
        for c in range(cdim // TN):
            cs = slice(c * TN, (c + 1) * TN)
            ext = jnp.concatenate([halo_ref[:, cs] * keep, z_ref[:, cs]], axis=0)
            conv = cw_ref[0:1, cs] * pltpu.roll(ext, width - 1, axis=0)[nhalo:]
            for k in range(1, width):
                sh = width - 1 - k
                zk = pltpu.roll(ext, sh, axis=0)[nhalo:] if sh else z_ref[:, cs]
                conv = conv + cw_ref[k:k + 1, cs] * zk
            p_scr[:, cs] = (bg_ref[:, cs] * conv).astype(BF16)

    @pl.when(i >= np_tiles)
    def _sample():
        hist = width - 1

        def zrow(r, cs):
            if r < hist:
                return st_ref[r * slab:(r + 1) * slab, cs]
            return z_ref[(r - hist) * slab:(r - hist + 1) * slab, cs]

        for c in range(cdim // TN):
            cs = slice(c * TN, (c + 1) * TN)
            for t in range(steps):
                conv = cw_ref[0:1, cs] * zrow(t, cs)
                for k in range(1, width):
                    conv = conv + cw_ref[k:k + 1, cs] * zrow(t + k, cs)
                rs = slice(t * slab, (t + 1) * slab)
                p_scr[rs, cs] = (bg_ref[rs, cs] * conv).astype(BF16)

    o_ref[...] = x_ref[...] + _bdot(p_scr[...], w_ref[0])


def _conv_out(bg, zc, x, state_t, conv_w, w_out, layer, *, m_p, seq, slab, steps):
    m, d = x.shape
    cd = zc.shape[1]
    width = conv_w.shape[0]
    assert m - m_p == TM == slab * steps and seq % TM == 0 and width - 1 <= SUBLANES
    assert w_out.dtype == BF16
    np_tiles = m_p // TM
    halo_blocks = TM // SUBLANES
    vm = (4 * TM * cd * 4 + 2 * SUBLANES * cd * 4 + (width - 1) * slab * cd * 4 + TM * cd * 2
          + cd * d * 2 + 4 * TM * d * 4 + 2 * TM * d * 4 + 6 * TM * TN * 4)
    kern = functools.partial(_conv_out_kernel, np_tiles=np_tiles, tiles_per_seq=seq // TM,
                             slab=slab, steps=steps)
    once = pl.Buffered(1)
    return pl.pallas_call(
        kern,
        grid=(m // TM,),
        in_specs=[
            pl.BlockSpec((TM, cd), lambda i: (i, 0)),
            pl.BlockSpec((TM, cd), lambda i: (i, 0)),
            pl.BlockSpec((SUBLANES, cd), lambda i: (jnp.maximum(i * halo_blocks - 1, 0), 0)),
            pl.BlockSpec(((width - 1) * slab, cd), lambda i: (0, 0), pipeline_mode=once),
            pl.BlockSpec((width, cd), lambda i: (0, 0)),
            pl.BlockSpec((1, cd, d), lambda i: (layer, 0, 0), pipeline_mode=once),
            pl.BlockSpec((TM, d), lambda i: (i, 0)),
        ],
        out_specs=pl.BlockSpec((TM, d), lambda i: (i, 0)),
        out_shape=jax.ShapeDtypeStruct((m, d), F32),
        scratch_shapes=[pltpu.VMEM((TM, cd), BF16)],
        compiler_params=pltpu.CompilerParams(
            dimension_semantics=("arbitrary",),
            vmem_limit_bytes=_vmem_limit(vm)),
        name="conv_out",
    )(bg, zc, zc, state_t, conv_w, w_out, x)


def _pool_kernel(x_ref, halo_ref, buf_ref, gain_ref, w_ref, sc_ref, o_ref, tail_ref, hs_ref, r_scr,
                 *, np_tiles, tiles_per_seq, slab, steps, windows, pos0):
    i = pl.program_id(0)
    tm = x_ref.shape[0]
    nhalo = halo_ref.shape[0]
    nbuf = buf_ref.shape[0] // slab
    gdim = buf_ref.shape[1]

    @pl.when(i <= np_tiles)
    def _():
        x = x_ref[...]
        r_scr[nhalo:, :] = lax.rsqrt(jnp.mean(x * x, axis=-1, keepdims=True) + EPS)
        hx = halo_ref[...]
        r_scr[:nhalo, :] = lax.rsqrt(jnp.mean(hx * hx, axis=-1, keepdims=True) + EPS)

    def normed(cs):
        return x_ref[:, cs] * r_scr[nhalo:, :] * gain_ref[:, cs]

    def finish(gi, cs, h, pooled):
        diff = (pooled - h).astype(BF16)
        y = _bdot(diff, w_ref[gi].astype(BF16)) * sc_ref[:, cs]
        o_ref[:, cs] = x_ref[:, cs] + y

    @pl.when(i < np_tiles)
    def _prompt():
        seq_tile = i % tiles_per_seq
        keep = (seq_tile != 0).astype(F32)
        pos = seq_tile * tm + lax.broadcasted_iota(jnp.int32, (tm, 1), 0)
        for gi, win in enumerate(windows):
            cs = slice(gi * gdim, (gi + 1) * gdim)
            h = normed(cs)
            tail_ref[:, cs] = h[tm - nhalo:]
            hh = halo_ref[:, cs] * r_scr[:nhalo, :] * gain_ref[:, cs] * keep
            s = jnp.concatenate([hh, h], axis=0)
            k = 1
            while k < win:
                s = s + pltpu.roll(s, k, axis=0)
                k *= 2
            count = jnp.minimum(win, pos + 1).astype(F32)
            finish(gi, cs, h, s[nhalo:] / count)

    for gi, win in enumerate(windows):
        @pl.when(i == np_tiles + gi)
        def _sample(gi=gi, win=win):
            cs = slice(gi * gdim, (gi + 1) * gdim)
            h = normed(cs)
            hs_ref[:, cs] = h

            def hrow(r):
                if r < nbuf:
                    return buf_ref[r * slab:(r + 1) * slab, :]
                return h[(r - nbuf) * slab:(r - nbuf + 1) * slab]

            rows = []
            for t in range(steps):
                acc = hrow(nbuf + t - win + 1)
                for r in range(nbuf + t - win + 2, nbuf + t + 1):
                    acc = acc + hrow(r)
                rows.append(acc / float(min(win, pos0 + t + 1)))
            finish(gi, cs, h, jnp.concatenate(rows, axis=0))


def _pool_mixer(x, gain, buf_t, w_group, scale, *, m_p, seq, slab, steps):
    m, d = x.shape
    ng, gdim = w_group.shape[0], w_group.shape[1]
    nhalo = 2 * SUBLANES
    assert ng == len(POOL_WINDOWS) and max(POOL_WINDOWS) <= nhalo and gdim * ng == d
    assert m - m_p == TM == slab * steps and seq % TM == 0
    assert buf_t.shape[0] // slab >= max(POOL_WINDOWS) - 1
    np_tiles = m_p // TM
    halo_blocks = TM // nhalo
    nbuf_rows = buf_t.shape[0]
    vm = (4 * TM * d * 4 + 2 * nhalo * d * 4 + 2 * nbuf_rows * gdim * 4 + ng * gdim * gdim * 4
          + gdim * gdim * 2 + TM * d * 4 + 2 * nhalo * d * 4 + (TM + nhalo) * LANES * 4 + 12 * TM * gdim * 4)
    kern = functools.partial(_pool_kernel, np_tiles=np_tiles, tiles_per_seq=seq // TM, slab=slab,
                             steps=steps, windows=POOL_WINDOWS, pos0=PAST_LEN)
    tile = lambda i: jnp.minimum(i, np_tiles)
    once = pl.Buffered(1)
    return pl.pallas_call(
        kern,
        grid=(np_tiles + ng,),
        in_specs=[
            pl.BlockSpec((TM, d), lambda i: (tile(i), 0)),
            pl.BlockSpec((nhalo, d), lambda i: (jnp.maximum(tile(i) * halo_blocks - 1, 0), 0)),
            pl.BlockSpec((nbuf_rows, gdim), lambda i: (0, jnp.maximum(i - np_tiles, 0))),
            pl.BlockSpec((1, d), lambda i: (0, 0)),
            pl.BlockSpec((ng, gdim, gdim), lambda i: (0, 0, 0), pipeline_mode=once),
            pl.BlockSpec((1, d), lambda i: (0, 0)),
        ],
        out_specs=[pl.BlockSpec((TM, d), lambda i: (tile(i), 0)),
                   pl.BlockSpec((nhalo, d), lambda i: (jnp.minimum(i, np_tiles - 1), 0)),
                   pl.BlockSpec((TM, d), lambda i: (0, 0), pipeline_mode=once)],
        out_shape=[jax.ShapeDtypeStruct((m, d), F32), jax.ShapeDtypeStruct((np_tiles * nhalo, d), F32),
                   jax.ShapeDtypeStruct((TM, d), F32)],
        scratch_shapes=[pltpu.VMEM((TM + nhalo, 1), F32)],
        compiler_params=pltpu.CompilerParams(
            dimension_semantics=("arbitrary",),
            vmem_limit_bytes=_vmem_limit(vm)),
        name="pool_mixer",
    )(x, x, buf_t, gain.reshape(1, d), w_group, scale.reshape(1, d))


def _ffn_kernel(ce_ref, nu_ref, nv_ref, xb_ref, x_ref, gain_ref, wg_hbm, wu_hbm, wd_hbm, o_ref,
                wg_buf, wu_buf, wd_buf, sems, maybe_xb_scr=None, *, dense, nf):
    del xb_ref
    c = pl.program_id(0)
    nch = pl.num_programs(0)
    rows = x_ref.shape[0]
    nvalid = nv_ref[c]
    units = nu_ref[c]

    def weight_copies(chunk, f, slot):
        e = ce_ref[chunk]
        cols = pl.ds(pl.multiple_of(f * FFN_TF, FFN_TF), FFN_TF)
        return (pltpu.make_async_copy(wg_hbm.at[e, :, cols], wg_buf.at[slot], sems.at[0, slot]),
                pltpu.make_async_copy(wu_hbm.at[e, :, cols], wu_buf.at[slot], sems.at[1, slot]),
                pltpu.make_async_copy(wd_hbm.at[e, cols, :], wd_buf.at[slot], sems.at[2, slot]))

    def start_weights(chunk, f, slot):
        for copy in weight_copies(chunk, f, slot):
            copy.start()

    def wait_weights(chunk, f, slot):
        for copy in weight_copies(chunk, f, slot):
            copy.wait()

    @pl.when(jnp.logical_and(c == 0, units > 0))
    def _prime():
        start_weights(0, 0, 0)

    for s in range(rows // FFN_UNIT):
        rs = slice(s * FFN_UNIT, (s + 1) * FFN_UNIT)
        if dense:
            x = x_ref[rs, :]
            live = (s * FFN_UNIT + lax.broadcasted_iota(jnp.int32, (FFN_UNIT, 1), 0)) < nvalid
            o_ref[rs, :] = x
            maybe_xb_scr[rs, :] = jnp.where(live, _rms(x, gain_ref[...]), 0.0).astype(BF16)
        else:
            o_ref[rs, :] = jnp.zeros((FFN_UNIT, o_ref.shape[1]), F32)

    def sub_tile(slot, start, nrows):
        rs = pl.ds(pl.multiple_of(start, FFN_UNIT), nrows)
        xs = maybe_xb_scr[rs, :] if dense else x_ref[rs, :].astype(BF16)
        gate = _bdot(xs, wg_buf[slot].astype(BF16))
        up = _bdot(xs, wu_buf[slot].astype(BF16))
        act = (gate * jax.nn.sigmoid(gate) * up).astype(BF16)
        o_ref[rs, :] += _bdot(act, wd_buf[slot].astype(BF16))

    def swiglu_block(slot):
        full = rows // FFN_UNIT
        straight = tuple(range(full - FFN_STRAIGHT + 1, full + 1))
        for n in straight:
            @pl.when(units == n)
            def _straight(n=n):
                sub_tile(slot, 0, n * FFN_UNIT)

        @pl.when(units < straight[0])
        def _partial():
            noct = lax.shift_right_logical(units, 3)

            def octet(s, carry):
                sub_tile(slot, s * (8 * FFN_UNIT), 8 * FFN_UNIT)
                return carry

            lax.fori_loop(0, noct, octet, 0)
            done = noct * 8
            for width in (4, 2, 1):
                bit = lax.bitwise_and(units, width)

                @pl.when(bit != 0)
                def _rest(width=width, done=done):
                    sub_tile(slot, done * FFN_UNIT, width * FFN_UNIT)

                done = done + bit

    @pl.when(units > 0)
    def _work():
        nxt = jnp.minimum(c + 1, nch - 1)
        next_live = jnp.logical_and(c + 1 < nch, nu_ref[nxt] > 0)

        def block(f, carry):
            slot = lax.bitwise_and(f, 1)
            wait_weights(c, f, slot)

            @pl.when(f + 1 < nf)
            def _():
                start_weights(c, f + 1, 1 - slot)

            @pl.when(jnp.logical_and(f + 1 == nf, next_live))
            def _():
                start_weights(nxt, 0, 1 - slot)

            swiglu_block(slot)
            return carry

        lax.fori_loop(0, nf, block, 0)


def _ffn(x, gain, wg, wu, wd, chunk_expert, chunk_units, chunk_nvalid, *, dense):
    m, d = x.shape
    dff = wg.shape[-1]
    nch = chunk_expert.shape[0]
    nf = dff // FFN_TF
    assert dff % FFN_TF == 0 and nf % 2 == 0
    last_live = jnp.maximum(jnp.sum(chunk_units > 0) - 1, 0).astype(jnp.int32)
    x_block = jnp.minimum(jnp.arange(nch, dtype=jnp.int32), last_live)
    vm = (2 * FFN_ROWS * d * 4 + FFN_ROWS * d * 2 + 3 * 2 * d * FFN_TF * 4
          + 3 * d * FFN_TF * 2 + 16 * FFN_UNIT * FFN_TF * 4)
    once = pl.Buffered(1)
    grid_spec = pltpu.PrefetchScalarGridSpec(
        num_scalar_prefetch=4,
        grid=(nch,),
        in_specs=[
            pl.BlockSpec((FFN_ROWS, d), lambda c, ce, nu, nv, xb: (xb[c], 0), pipeline_mode=once),
            pl.BlockSpec((1, d), lambda c, ce, nu, nv, xb: (0, 0)),
            pl.BlockSpec(memory_space=pl.ANY),
            pl.BlockSpec(memory_space=pl.ANY),
            pl.BlockSpec(memory_space=pl.ANY),
        ],
        out_specs=pl.BlockSpec((FFN_ROWS, d), lambda c, ce, nu, nv, xb: (c, 0),
                               pipeline_mode=once if dense else None),
        scratch_shapes=[pltpu.VMEM((2, d, FFN_TF), F32), pltpu.VMEM((2, d, FFN_TF), F32),
                        pltpu.VMEM((2, FFN_TF, d), F32), pltpu.SemaphoreType.DMA((3, 2))]
        + ([pltpu.VMEM((FFN_ROWS, d), BF16)] if dense else []),
    )
    return pl.pallas_call(
        functools.partial(_ffn_kernel, dense=dense, nf=nf),
        grid_spec=grid_spec,
        out_shape=jax.ShapeDtypeStruct((m, d), F32),
        compiler_params=pltpu.CompilerParams(
            dimension_semantics=("arbitrary",),
            vmem_limit_bytes=_vmem_limit(vm)),
        name="ffn_dense" if dense else "ffn_experts",
    )(chunk_expert, chunk_units, chunk_nvalid, x_block, x, gain.reshape(1, d), wg, wu, wd)


def _dense_ffn(x, gain, wg, wu, wd, layer):
    m = x.shape[0]
    nch = pl.cdiv(m, FFN_ROWS)
    nvalid = jnp.minimum(FFN_ROWS, m - FFN_ROWS * jnp.arange(nch, dtype=jnp.int32))
    units = (nvalid + FFN_UNIT - 1) // FFN_UNIT
    which = jnp.full((nch,), layer, jnp.int32)
    return _ffn(x, gain, wg, wu, wd, which, units, nvalid, dense=True)


def _route_kernel(x_ref, g_ref, r_ref, hn_ref, info_ref, fields_ref, cnt_ref, carry_scr, *, n_experts):
    i = pl.program_id(0)
    tm = x_ref.shape[0]

    @pl.when(i == 0)
    def _():
        carry_scr[...] = jnp.zeros_like(carry_scr)

    hn = _rms(x_ref[...], g_ref[...])
    hn_ref[...] = hn
    lane = lax.broadcasted_iota(jnp.int32, (tm, LANES), 1).astype(F32)
    logits = jnp.where(lane < n_experts, _split_dot(hn, r_ref[...]), -jnp.inf)
    m1 = jnp.max(logits, axis=-1, keepdims=True)
    i1 = jnp.min(jnp.where(logits == m1, lane, float(LANES)), axis=-1, keepdims=True)
    rest = jnp.where(lane == i1, -jnp.inf, logits)
    m2 = jnp.max(rest, axis=-1, keepdims=True)
    i2 = jnp.min(jnp.where(rest == m2, lane, float(LANES)), axis=-1, keepdims=True)
    e = jnp.exp(m2 - m1)
    g1 = 1.0 / (1.0 + e)
    g2 = e / (1.0 + e)
    oh1 = (lane == i1).astype(F32)
    oh2 = (lane == i2).astype(F32)
    cnt = oh1 + oh2
    row = lax.broadcasted_iota(jnp.int32, (tm, tm), 0)
    col = lax.broadcasted_iota(jnp.int32, (tm, tm), 1)
    before = (col < row).astype(BF16)
    ranks = _bdot(before, cnt.astype(BF16)) + carry_scr[0:1, :]
    rank1 = jnp.sum(ranks * oh1, axis=-1, keepdims=True)
    rank2 = jnp.sum(ranks * oh2, axis=-1, keepdims=True)
    carry_scr[0:1, :] = carry_scr[0:1, :] + jnp.sum(cnt, axis=0, keepdims=True)
    cnt_ref[...] = jnp.broadcast_to(carry_scr[0:1, :], cnt_ref.shape)
    info = jnp.where(lane == 0, i1, 0.0)
    info = jnp.where(lane == 1, i2, info)
    info = jnp.where(lane == 2, g1, info)
    info = jnp.where(lane == 3, g2, info)
    info = jnp.where(lane == 4, rank1, info)
    info = jnp.where(lane == 5, rank2, info)
    info_ref[...] = info
    fields_ref[...] = info.T[:fields_ref.shape[0]]


def _route(x, gain, router):
    m, d = x.shape
    ne = router.shape[1]
    rpad = jnp.pad(router, ((0, 0), (0, LANES - ne)))
    vm = 4 * TM * d * 4 + 2 * d * LANES * 4 + 4 * TM * LANES * 4 + TM * TM * 8 + 8 * TM * d * 4
    return pl.pallas_call(
        functools.partial(_route_kernel, n_experts=ne),
        grid=(m // TM,),
        in_specs=[
            pl.BlockSpec((TM, d), lambda i: (i, 0)),
            pl.BlockSpec((1, d), lambda i: (0, 0)),
            pl.BlockSpec((d, LANES), lambda i: (0, 0)),
        ],
        out_specs=[pl.BlockSpec((TM, d), lambda i: (i, 0)),
                   pl.BlockSpec((TM, LANES), lambda i: (i, 0)),
                   pl.BlockSpec((SUBLANES, TM), lambda i: (0, i)),
                   pl.BlockSpec((SUBLANES, LANES), lambda i: (0, 0))],
        out_shape=[jax.ShapeDtypeStruct((m, d), F32), jax.ShapeDtypeStruct((m, LANES), F32),
                   jax.ShapeDtypeStruct((SUBLANES, m), F32), jax.ShapeDtypeStruct((SUBLANES, LANES), F32)],
        scratch_shapes=[pltpu.VMEM((SUBLANES, LANES), F32)],
        compiler_params=pltpu.CompilerParams(
            dimension_semantics=("arbitrary",), vmem_limit_bytes=_vmem_limit(vm)),
        name="route",
    )(x, gain.reshape(1, d), rpad)


def _row_copy(src, src_row, dst, dst_row, sem):
    return pltpu.make_async_copy(src.at[pl.ds(src_row, 1), :], dst.at[pl.ds(dst_row, 1), :], sem)


def _dispatch_kernel(p1_ref, p2_ref, hn_ref, init_ref, xs_ref, sem):
    del init_ref
    base = pl.program_id(0) * hn_ref.shape[0]
    tm = hn_ref.shape[0]

    def start(r, carry):
        _row_copy(hn_ref, r, xs_ref, p1_ref[base + r], sem).start()
        _row_copy(hn_ref, r, xs_ref, p2_ref[base + r], sem).start()
        return carry

    def wait(r, carry):
        _row_copy(hn_ref, r, xs_ref, p1_ref[base + r], sem).wait()
        _row_copy(hn_ref, r, xs_ref, p2_ref[base + r], sem).wait()
        return carry

    lax.fori_loop(0, tm, start, 0, unroll=ROW_DMA_UNROLL)
    lax.fori_loop(0, tm, wait, 0, unroll=ROW_DMA_UNROLL)


def _dispatch(hn, pos1, pos2, init):
    m, d = hn.shape
    n_rows = init.shape[0]
    assert init.shape == (n_rows, d) and init.dtype == F32
    grid_spec = pltpu.PrefetchScalarGridSpec(
        num_scalar_prefetch=2,
        grid=(m // ROUTE_TM,),
        in_specs=[pl.BlockSpec((ROUTE_TM, d), lambda i, p1, p2: (i, 0)),
                  pl.BlockSpec(memory_space=pl.ANY)],
        out_specs=pl.BlockSpec(memory_space=pl.ANY),
        scratch_shapes=[pltpu.SemaphoreType.DMA(())],
    )
    return pl.pallas_call(
        _dispatch_kernel,
        grid_spec=grid_spec,
        out_shape=jax.ShapeDtypeStruct((n_rows, d), F32),
        input_output_aliases={3: 0},
        compiler_params=pltpu.CompilerParams(dimension_semantics=("arbitrary",)),
        name="dispatch",
    )(pos1, pos2, hn, init)


def _combine_kernel(p1_ref, p2_ref, x_ref, info_ref, gain_ref, ys_ref, *refs, final_norm, head_tiles):
    *out_refs, b1_scr, b2_scr, sems = refs
    tm = x_ref.shape[0]
    i = pl.program_id(0)
    slot = lax.bitwise_and(i, 1)

    def gather(tile, sl, go):
        def row(r, carry):
            t = tile * tm + r
            for pos_ref, buf in ((p1_ref, b1_scr), (p2_ref, b2_scr)):
                copy = _row_copy(ys_ref, pos_ref[t], buf.at[sl], r, sems.at[sl])
                copy.start() if go else copy.wait()
            return carry

        lax.fori_loop(0, tm, row, 0, unroll=ROW_DMA_UNROLL)

    @pl.when(i == 0)
    def _():
        gather(0, 0, True)

    @pl.when(i + 1 < pl.num_programs(0))
    def _():
        gather(i + 1, 1 - slot, True)

    gather(i, slot, False)
    info = info_ref[...]
    e1, e2 = info[:, 0:1], info[:, 1:2]
    g1, g2 = info[:, 2:3], info[:, 3:4]
    y1 = g1 * b1_scr[slot]
    y2 = g2 * b2_scr[slot]
    lo = jnp.where(e1 < e2, y1, y2)
    hi = jnp.where(e1 < e2, y2, y1)
    out = x_ref[...] + (lo + hi)
    if final_norm:
        out = _rms(out, gain_ref[...])
    if head_tiles is None:
        out_refs[0][...] = out
    else:
        head_ref, tail_ref = out_refs

        @pl.when(i < head_tiles)
        def _():
            head_ref[...] = out

        @pl.when(i >= head_tiles)
        def _():
            tail_ref[...] = out


def _combine(x, info, ys, pos1, pos2, gain, *, final_norm, head_rows=None):
    m, d = x.shape
    vm = 6 * ROUTE_TM * d * 4 + 2 * ROUTE_TM * LANES * 4 + 4 * ROUTE_TM * d * 4 + 6 * ROUTE_TM * d * 4
    if head_rows is None:
        head_tiles = None
        out_specs = pl.BlockSpec((ROUTE_TM, d), lambda i, p1, p2: (i, 0))
        out_shape = jax.ShapeDtypeStruct((m, d), F32)
    else:
        assert head_rows % ROUTE_TM == 0 and 0 < head_rows < m
        head_tiles = head_rows // ROUTE_TM
        out_specs = [pl.BlockSpec((ROUTE_TM, d), lambda i, p1, p2: (jnp.minimum(i, head_tiles - 1), 0)),
                     pl.BlockSpec((ROUTE_TM, d), lambda i, p1, p2: (jnp.maximum(i - head_tiles, 0), 0))]
        out_shape = [jax.ShapeDtypeStruct((head_rows, d), F32), jax.ShapeDtypeStruct((m - head_rows, d), F32)]
    grid_spec = pltpu.PrefetchScalarGridSpec(
        num_scalar_prefetch=2,
        grid=(m // ROUTE_TM,),
        in_specs=[
            pl.BlockSpec((ROUTE_TM, d), lambda i, p1, p2: (i, 0)),
            pl.BlockSpec((ROUTE_TM, LANES), lambda i, p1, p2: (i, 0)),
            pl.BlockSpec((1, d), lambda i, p1, p2: (0, 0)),
            pl.BlockSpec(memory_space=pl.ANY),
        ],
        out_specs=out_specs,
        scratch_shapes=[pltpu.VMEM((2, ROUTE_TM, d), F32), pltpu.VMEM((2, ROUTE_TM, d), F32),
                        pltpu.SemaphoreType.DMA((2,))],
    )
    return pl.pallas_call(
        functools.partial(_combine_kernel, final_norm=final_norm, head_tiles=head_tiles),
        grid_spec=grid_spec,
        out_shape=out_shape,
        compiler_params=pltpu.CompilerParams(
            dimension_semantics=("arbitrary",), vmem_limit_bytes=_vmem_limit(vm)),
        name="combine_final" if final_norm else "combine",
    )(pos1, pos2, x, info, gain.reshape(1, d), ys)


def _moe_ffn(x, gain, router, wg, wu, wd, layer, out_gain, *, final_norm, head_rows=None, spare=None):
    m, d = x.shape
    ne = router.shape[1]
    wg, wu, wd = (w.reshape((-1,) + w.shape[2:]) for w in (wg, wu, wd))
    assert m % TM == 0 and m % ROUTE_TM == 0
    hn, info, fields, cnt = _route(x, gain, router)
    fields = fields.astype(jnp.int32)
    counts = cnt[0, :ne].astype(jnp.int32)
    chunks_per = (counts + FFN_ROWS - 1) // FFN_ROWS
    chunk_end = jnp.cumsum(chunks_per)
    chunk_start = chunk_end - chunks_per
    share = (counts + jnp.maximum(chunks_per, 1) - 1) // jnp.maximum(chunks_per, 1)
    share = jnp.maximum((share + FFN_UNIT - 1) // FFN_UNIT * FFN_UNIT, FFN_UNIT)

    def position(e, rank):
        j = rank // share[e]
        return (chunk_start[e] + j) * FFN_ROWS + rank - j * share[e]

    pos1 = position(fields[0], fields[4])
    pos2 = position(fields[1], fields[5])
    nch = (m * TOP_K) // FFN_ROWS + ne
    cidx = jnp.arange(nch, dtype=jnp.int32)
    last = jnp.maximum(chunk_end[-1] - 1, 0)
    owner = jnp.sum(jnp.minimum(cidx, last)[:, None] >= chunk_end[None, :], axis=1).astype(jnp.int32)
    owner = jnp.minimum(owner, ne - 1)
    local = cidx - chunk_start[owner]
    nvalid = jnp.clip(counts[owner] - local * share[owner], 0, share[owner])
    nvalid = jnp.where(cidx < chunk_end[-1], nvalid, 0).astype(jnp.int32)
    units = (nvalid + FFN_UNIT - 1) // FFN_UNIT
    if spare is None:
        spare = jnp.zeros((nch * FFN_ROWS, d), F32)
    xs = _dispatch(hn, pos1, pos2, spare)
    ys = _ffn(xs, gain, wg, wu, wd, owner + layer * ne, units, nvalid, dense=False)
    out = _combine(x, info, ys, pos1, pos2, out_gain, final_norm=final_norm, head_rows=head_rows)
    return out, ys


def kernel(x_prompt, x_sample, state_conv, state_pool, norm_mix, norm_ffn, final_norm, a_w_in, a_v_gain, a_w_s, a_b_s, a_w_out, b_w_in, b_conv, b_w_out, c_w_group, c_scale, ffn_w_gate, ffn_w_up, ffn_w_down, moe_router, moe_w_gate, moe_w_up, moe_w_down):
    batch, seq, d = x_prompt.shape
    slab, steps, _ = x_sample.shape
    depth = norm_mix.shape[0]
    assert depth % 2 == 0, "the final RMSNorm is fused into the routed combine kernel of the last layer"
    m_p = batch * seq
    geo = dict(m_p=m_p, slab=slab, steps=steps)

    def to_time_major(a):
        return a.transpose(1, 0, 2).reshape(a.shape[1] * slab, a.shape[2])

    def from_time_major(a, r):
        return a.reshape(r, slab, a.shape[-1]).transpose(1, 0, 2)

    def seq_tails(a, r):
        return jnp.stack([a[(b + 1) * seq - r:(b + 1) * seq] for b in range(batch)])

    a_w_out_b = a_w_out.astype(BF16)
    b_w_out_b = b_w_out.astype(BF16)
    x = jnp.concatenate([x_prompt.reshape(m_p, d), to_time_major(x_sample)], axis=0)
    chunk_v_s, conv_p, conv_s, pool_p, pool_s = [], [], [], [], []
    spare = None
    for i in range(depth):
        j, kind = divmod(i, 3)
        if kind == 0:
            z = _inproj_gelu(x, norm_mix[i], a_w_in, j)
            x, cv = _chunk_out(z, x, a_v_gain[j], a_w_s[j], a_b_s[j], a_w_out_b, j, **geo)
            chunk_v_s.append(from_time_major(cv, steps))
        elif kind == 1:
            hist = b_conv.shape[1] - 1
            bg, zc = _inproj_conv(x, norm_mix[i], b_w_in, j)
            x = _conv_out(bg, zc, x, to_time_major(state_conv[j]), b_conv[j], b_w_out_b, j, seq=seq, **geo)
            conv_p.append(seq_tails(zc, hist))
            conv_s.append(from_time_major(zc[m_p:], steps)[:, steps - hist:])
        else:
            nbuf = state_pool.shape[2]
            x, tails, hn_s = _pool_mixer(x, norm_mix[i], to_time_major(state_pool[j]), c_w_group[j],
                                         c_scale[j], seq=seq, **geo)
            per_tile = tails.shape[0] // (m_p // TM)
            ends = tails.reshape(batch, seq // TM, per_tile, d)[:, -1]
            pool_p.append(ends[:, per_tile - nbuf:])
            hc = jnp.concatenate([state_pool[j], from_time_major(hn_s, steps)], axis=1)
            pool_s.append(hc[:, hc.shape[1] - nbuf:])
        f = i // 2
        if i % 2 == 0:
            x = _dense_ffn(x, norm_ffn[i], ffn_w_gate, ffn_w_up, ffn_w_down, f)
        else:
            last = i == depth - 1
            x, spare = _moe_ffn(x, norm_ffn[i], moe_router[f], moe_w_gate, moe_w_up, moe_w_down, f,
                                final_norm, final_norm=last, head_rows=m_p if last else None, spare=spare)
    y_prompt = x[0].reshape(batch, seq, d)
    y_sample = from_time_major(x[1], steps)
    return (y_prompt, y_sample, jnp.stack(chunk_v_s), jnp.stack(conv_p), jnp.stack(conv_s),
            jnp.stack(pool_p), jnp.stack(pool_s))
```

```python
import functools
import math

import jax
import jax.numpy as jnp
from jax import lax
from jax.experimental import pallas as pl
from jax.experimental.pallas import tpu as pltpu

F32 = jnp.float32
BF16 = jnp.bfloat16
EPS = 1e-6
PAST_LEN = 16384
POOL_WINDOWS = (2, 4, 8, 16)
TOP_K = 2

V7X_VMEM_BYTES = 64 * 1024 * 1024
SUBLANES = 8
LANES = 128

TM = 512
TN = 512
FFN_UNIT = 32
FFN_ROWS = 36 * FFN_UNIT
FFN_STRAIGHT = 4
FFN_GROUP = 16
FFN_TF = 512
ROUTE_TM = 512
ROW_DMA_UNROLL = 32


def _vmem_limit(nbytes):
    return min(int(nbytes) + (6 << 20), V7X_VMEM_BYTES - (4 << 20))


def _rms(x, gain):
    ms = jnp.mean(x * x, axis=-1, keepdims=True)
    return x * lax.rsqrt(ms + EPS) * gain


def _gelu_tanh(x):
    c = math.sqrt(2.0 / math.pi)
    return 0.5 * x * (1.0 + jnp.tanh(c * (x + 0.044715 * (x * x * x))))


def _bdot(a, b):
    return jnp.dot(a, b, preferred_element_type=F32)


def _split_dot(a, b):
    ah = a.astype(BF16)
    al = (a - ah.astype(F32)).astype(BF16)
    bh = b.astype(BF16)
    bl = (b - bh.astype(F32)).astype(BF16)
    return _bdot(ah, bh) + _bdot(ah, bl) + _bdot(al, bh)


INPROJ_ROW_TILES = 8


def _inproj_rows(m):
    assert m % (INPROJ_ROW_TILES * 2 * SUBLANES) == 0
    return m // INPROJ_ROW_TILES


def _inproj_gelu_kernel(x_ref, g_ref, w_ref, o_ref, h_scr):
    @pl.when(pl.program_id(1) == 0)
    def _():
        h_scr[...] = _rms(x_ref[...], g_ref[...]).astype(BF16)

    z = _bdot(h_scr[...], w_ref[0].astype(BF16))
    o_ref[...] = _gelu_tanh(z)


def _inproj_gelu(x, gain, w, layer):
    m, d = x.shape
    n = w.shape[2]
    tm = _inproj_rows(m)
    tn = 2 * TN
    vm = 2 * tm * d * 4 + tm * d * 2 + 2 * d * tn * 4 + d * tn * 2 + 4 * tm * tn * 4
    return pl.pallas_call(
        _inproj_gelu_kernel,
        grid=(m // tm, n // tn),
        in_specs=[
            pl.BlockSpec((tm, d), lambda i, j: (i, 0)),
            pl.BlockSpec((1, d), lambda i, j: (0, 0)),
            pl.BlockSpec((1, d, tn), lambda i, j: (layer, 0, j)),
        ],
        out_specs=pl.BlockSpec((tm, tn), lambda i, j: (i, j)),
        out_shape=jax.ShapeDtypeStruct((m, n), F32),
        scratch_shapes=[pltpu.VMEM((tm, d), BF16)],
        compiler_params=pltpu.CompilerParams(
            dimension_semantics=("arbitrary", "arbitrary"),
            vmem_limit_bytes=_vmem_limit(vm)),
        name="inproj_gelu",
    )(x, gain.reshape(1, d), w)


def _inproj_conv_kernel(x_ref, g_ref, wb_ref, wc_ref, wh_ref, ob_ref, oz_ref, h_scr):
    @pl.when(pl.program_id(1) == 0)
    def _():
        h_scr[...] = _rms(x_ref[...], g_ref[...]).astype(BF16)

    h = h_scr[...]
    ob_ref[...] = _bdot(h, wb_ref[0].astype(BF16))
    c = _bdot(h, wc_ref[0].astype(BF16))
    hin = _bdot(h, wh_ref[0].astype(BF16))
    oz_ref[...] = c * hin


def _inproj_conv(x, gain, w, layer):
    m, d = x.shape
    cd = w.shape[2] // 3
    tm = _inproj_rows(m)
    tn = TN // 2
    nb = cd // tn
    vm = 2 * tm * d * 4 + tm * d * 2 + 3 * (2 * d * tn * 4 + d * tn * 2) + 10 * tm * tn * 4
    out = jax.ShapeDtypeStruct((m, cd), F32)
    return pl.pallas_call(
        _inproj_conv_kernel,
        grid=(m // tm, nb),
        in_specs=[
            pl.BlockSpec((tm, d), lambda i, j: (i, 0)),
            pl.BlockSpec((1, d), lambda i, j: (0, 0)),
            pl.BlockSpec((1, d, tn), lambda i, j: (layer, 0, j)),
            pl.BlockSpec((1, d, tn), lambda i, j: (layer, 0, j + nb)),
            pl.BlockSpec((1, d, tn), lambda i, j: (layer, 0, j + 2 * nb)),
        ],
        out_specs=[pl.BlockSpec((tm, tn), lambda i, j: (i, j)),
                   pl.BlockSpec((tm, tn), lambda i, j: (i, j))],
        out_shape=[out, out],
        scratch_shapes=[pltpu.VMEM((tm, d), BF16)],
        compiler_params=pltpu.CompilerParams(
            dimension_semantics=("arbitrary", "arbitrary"),
            vmem_limit_bytes=_vmem_limit(vm)),
        name="inproj_conv",
    )(x, gain.reshape(1, d), w, w, w)


def _chunk_out_kernel(u_ref, v_ref, vg_ref, gw_ref, gb_ref, sw_ref, sb_ref, w_ref, x_ref,
                      o_ref, cv_ref, p_scr, *, np_tiles, chunk, groups, slab, steps):
    i = pl.program_id(0)
    gd = p_scr.shape[1] // groups

    @pl.when(i < np_tiles)
    def _prompt():
        row = lax.broadcasted_iota(jnp.int32, (chunk, chunk), 0)
        col = lax.broadcasted_iota(jnp.int32, (chunk, chunk), 1)
        causal = col <= row
        for s in range(p_scr.shape[0] // chunk):
            rs = slice(s * chunk, (s + 1) * chunk)
            vb = _rms(v_ref[rs, :], vg_ref[...]).astype(BF16)
            for h in range(groups):
                cs = slice(h * gd, (h + 1) * gd)
                wm = jnp.where(causal, gw_ref[h], 0.0).astype(BF16)
                mixed = _bdot(wm, vb[:, cs]) + gb_ref[:, h:h + 1]
                p_scr[rs, cs] = (u_ref[rs, cs] * mixed).astype(BF16)

    @pl.when(i >= np_tiles)
    def _sample():
        for t in range(steps):
            rs = slice(t * slab, (t + 1) * slab)
            cv_ref[rs, :] = _rms(v_ref[rs, :], vg_ref[...])
        for t in range(steps):
            rs = slice(t * slab, (t + 1) * slab)
            for h in range(groups):
                cs = slice(h * gd, (h + 1) * gd)
                mixed = sw_ref[t * steps:t * steps + 1, cs] * cv_ref[0:slab, cs]
                for s in range(1, t + 1):
                    k = t * steps + s
                    mixed = mixed + sw_ref[k:k + 1, cs] * cv_ref[s * slab:(s + 1) * slab, cs]
                mixed = mixed + sb_ref[t:t + 1, cs]
                p_scr[rs, cs] = (u_ref[rs, cs] * mixed).astype(BF16)

    o_ref[...] = x_ref[...] + _bdot(p_scr[...], w_ref[0])


def _chunk_out(z, x, v_gain, w_s, b_s, w_out, layer, *, m_p, slab, steps):
    m, d = x.shape
    a = z.shape[1] // 2
    groups, chunk = w_s.shape[0], w_s.shape[1]
    gd = a // groups
    assert m_p % TM == 0 and TM % chunk == 0 and m - m_p == TM == slab * steps
    assert steps <= chunk and slab % SUBLANES == 0 and w_out.dtype == BF16
    np_tiles = m_p // TM
    gb = b_s.T
    sw = jnp.repeat(w_s[:, :steps, :steps].transpose(1, 2, 0).reshape(steps * steps, groups), gd, axis=1)
    sb = jnp.repeat(b_s[:, :steps].T, gd, axis=1)
    vm = (4 * TM * a * 4 + TM * a * 2 + 2 * groups * chunk * chunk * 4 + a * d * 2
          + 4 * TM * d * 4 + TM * a * 4 + 4 * (steps * steps + steps) * a * 4 + 2 * TM * d * 4)
    once = pl.Buffered(1)
    kern = functools.partial(_chunk_out_kernel, np_tiles=np_tiles, chunk=chunk, groups=groups,
                             slab=slab, steps=steps)
    return pl.pallas_call(
        kern,
        grid=(m // TM,),
        in_specs=[
            pl.BlockSpec((TM, a), lambda i: (i, 0)),
            pl.BlockSpec((TM, a), lambda i: (i, 1)),
            pl.BlockSpec((1, a), lambda i: (0, 0)),
            pl.BlockSpec((groups, chunk, chunk), lambda i: (0, 0, 0)),
            pl.BlockSpec((chunk, groups), lambda i: (0, 0)),
            pl.BlockSpec((steps * steps, a), lambda i: (0, 0)),
            pl.BlockSpec((steps, a), lambda i: (0, 0)),
            pl.BlockSpec((1, a, d), lambda i: (layer, 0, 0), pipeline_mode=once),
            pl.BlockSpec((TM, d), lambda i: (i, 0)),
        ],
        out_specs=[pl.BlockSpec((TM, d), lambda i: (i, 0)),
                   pl.BlockSpec((TM, a), lambda i: (0, 0), pipeline_mode=once)],
        out_shape=[jax.ShapeDtypeStruct((m, d), F32), jax.ShapeDtypeStruct((TM, a), F32)],
        scratch_shapes=[pltpu.VMEM((TM, a), BF16)],
        compiler_params=pltpu.CompilerParams(
            dimension_semantics=("arbitrary",),
            vmem_limit_bytes=_vmem_limit(vm)),
        name="chunk_out",
    )(z, z, v_gain.reshape(1, a), w_s, gb, sw, sb, w_out, x)


def _conv_out_kernel(bg_ref, z_ref, halo_ref, st_ref, cw_ref, w_ref, x_ref, o_ref, p_scr,
                     *, np_tiles, tiles_per_seq, slab, steps):
    i = pl.program_id(0)
    width = cw_ref.shape[0]
    nhalo = halo_ref.shape[0]
    tm, cdim = p_scr.shape

    @pl.when(i < np_tiles)
    def _prompt():
        keep = (i % tiles_per_seq != 0).astype(F32)
        for c in range(cdim // TN):
            cs = slice(c * TN, (c + 1) * TN)
            ext = jnp.concatenate([halo_ref[:, cs] * keep, z_ref[:, cs]], axis=0)
            conv = cw_ref[0:1, cs] * pltpu.roll(ext, width - 1, axis=0)[nhalo:]
            for k in range(1, width):
                sh = width - 1 - k
                zk = pltpu.roll(ext, sh, axis=0)[nhalo:] if sh else z_ref[:, cs]
                conv = conv + cw_ref[k:k + 1, cs] * zk
            p_scr[:, cs] = (bg_ref[:, cs] * conv).astype(BF16)

    @pl.when(i >= np_tiles)
    def _sample():
        hist = width - 1

        def zrow(r, cs):
            if r < hist:
                return st_ref[r * slab:(r + 1) * slab, cs]
            return z_ref[(r - hist) * slab:(r - hist + 1) * slab, cs]

        for c in range(cdim // TN):
            cs = slice(c * TN, (c + 1) * TN)
            for t in range(steps):
                conv = cw_ref[0:1, cs] * zrow(t, cs)
                for k in range(1, width):
                    conv = conv + cw_ref[k:k + 1, cs] * zrow(t + k, cs)
                rs = slice(t * slab, (t + 1) * slab)
                p_scr[rs, cs] = (bg_ref[rs, cs] * conv).astype(BF16)

    o_ref[...] = x_ref[...] + _bdot(p_scr[...], w_ref[0])


def _conv_out(bg, zc, x, state_t, conv_w, w_out, layer, *, m_p, seq, slab, steps):
    m, d = x.shape
    cd = zc.shape[1]
    width = conv_w.shape[0]
    assert m - m_p == TM == slab * steps and seq % TM == 0 and width - 1 <= SUBLANES
    assert w_out.dtype == BF16
    np_tiles = m_p // TM
    halo_blocks = TM // SUBLANES
    vm = (4 * TM * cd * 4 + 2 * SUBLANES * cd * 4 + (width - 1) * slab * cd * 4 + TM * cd * 2
          + cd * d * 2 + 4 * TM * d * 4 + 2 * TM * d * 4 + 6 * TM * TN * 4)
    kern = functools.partial(_conv_out_kernel, np_tiles=np_tiles, tiles_per_seq=seq // TM,
                             slab=slab, steps=steps)
    once = pl.Buffered(1)
    return pl.pallas_call(
        kern,
        grid=(m // TM,),
        in_specs=[
            pl.BlockSpec((TM, cd), lambda i: (i, 0)),
            pl.BlockSpec((TM, cd), lambda i: (i, 0)),
            pl.BlockSpec((SUBLANES, cd), lambda i: (jnp.maximum(i * halo_blocks - 1, 0), 0)),
            pl.BlockSpec(((width - 1) * slab, cd), lambda i: (0, 0), pipeline_mode=once),
            pl.BlockSpec((width, cd), lambda i: (0, 0)),
            pl.BlockSpec((1, cd, d), lambda i: (layer, 0, 0), pipeline_mode=once),
            pl.BlockSpec((TM, d), lambda i: (i, 0)),
        ],
        out_specs=pl.BlockSpec((TM, d), lambda i: (i, 0)),
        out_shape=jax.ShapeDtypeStruct((m, d), F32),
        scratch_shapes=[pltpu.VMEM((TM, cd), BF16)],
        compiler_params=pltpu.CompilerParams(
            dimension_semantics=("arbitrary",),
            vmem_limit_bytes=_vmem_limit(vm)),
        name="conv_out",
    )(bg, zc, zc, state_t, conv_w, w_out, x)


def _pool_kernel(x_ref, halo_ref, buf_ref, gain_ref, w_ref, sc_ref, o_ref, tail_ref, hs_ref, r_scr,
                 *, np_tiles, tiles_per_seq, slab, steps, windows, pos0):
    i = pl.program_id(0)
    tm = x_ref.shape[0]
    nhalo = halo_ref.shape[0]
    nbuf = buf_ref.shape[0] // slab
    gdim = buf_ref.shape[1]

    @pl.when(i <= np_tiles)
    def _():
        x = x_ref[...]
        r_scr[nhalo:, :] = lax.rsqrt(jnp.mean(x * x, axis=-1, keepdims=True) + EPS)
        hx = halo_ref[...]
        r_scr[:nhalo, :] = lax.rsqrt(jnp.mean(hx * hx, axis=-1, keepdims=True) + EPS)

    def normed(cs):
        return x_ref[:, cs] * r_scr[nhalo:, :] * gain_ref[:, cs]

    def finish(gi, cs, h, pooled):
        diff = (pooled - h).astype(BF16)
        y = _bdot(diff, w_ref[gi].astype(BF16)) * sc_ref[:, cs]
        o_ref[:, cs] = x_ref[:, cs] + y

    @pl.when(i < np_tiles)
    def _prompt():
        seq_tile = i % tiles_per_seq
        keep = (seq_tile != 0).astype(F32)
        pos = seq_tile * tm + lax.broadcasted_iota(jnp.int32, (tm, 1), 0)
        for gi, win in enumerate(windows):
            cs = slice(gi * gdim, (gi + 1) * gdim)
            h = normed(cs)
            tail_ref[:, cs] = h[tm - nhalo:]
            hh = halo_ref[:, cs] * r_scr[:nhalo, :] * gain_ref[:, cs] * keep
            s = jnp.concatenate([hh, h], axis=0)
            k = 1
            while k < win:
                s = s + pltpu.roll(s, k, axis=0)
                k *= 2
            count = jnp.minimum(win, pos + 1).astype(F32)
            finish(gi, cs, h, s[nhalo:] / count)

    for gi, win in enumerate(windows):
        @pl.when(i == np_tiles + gi)
        def _sample(gi=gi, win=win):
            cs = slice(gi * gdim, (gi + 1) * gdim)
            h = normed(cs)
            hs_ref[:, cs] = h

            def hrow(r):
                if r < nbuf:
                    return buf_ref[r * slab:(r + 1) * slab, :]
                return h[(r - nbuf) * slab:(r - nbuf + 1) * slab]

            rows = []
            for t in range(steps):
                acc = hrow(nbuf + t - win + 1)
                for r in range(nbuf + t - win + 2, nbuf + t + 1):
                    acc = acc + hrow(r)
                rows.append(acc / float(min(win, pos0 + t + 1)))
            finish(gi, cs, h, jnp.concatenate(rows, axis=0))


def _pool_mixer(x, gain, buf_t, w_group, scale, *, m_p, seq, slab, steps):
    m, d = x.shape
    ng, gdim = w_group.shape[0], w_group.shape[1]
    nhalo = 2 * SUBLANES
    assert ng == len(POOL_WINDOWS) and max(POOL_WINDOWS) <= nhalo and gdim * ng == d
    assert m - m_p == TM == slab * steps and seq % TM == 0
    assert buf_t.shape[0] // slab >= max(POOL_WINDOWS) - 1
    np_tiles = m_p // TM
    halo_blocks = TM // nhalo
    nbuf_rows = buf_t.shape[0]
    vm = (4 * TM * d * 4 + 2 * nhalo * d * 4 + 2 * nbuf_rows * gdim * 4 + ng * gdim * gdim * 4
          + gdim * gdim * 2 + TM * d * 4 + 2 * nhalo * d * 4 + (TM + nhalo) * LANES * 4 + 12 * TM * gdim * 4)
    kern = functools.partial(_pool_kernel, np_tiles=np_tiles, tiles_per_seq=seq // TM, slab=slab,
                             steps=steps, windows=POOL_WINDOWS, pos0=PAST_LEN)
    tile = lambda i: jnp.minimum(i, np_tiles)
    once = pl.Buffered(1)
    return pl.pallas_call(
        kern,
        grid=(np_tiles + ng,),
        in_specs=[
            pl.BlockSpec((TM, d), lambda i: (tile(i), 0)),
            pl.BlockSpec((nhalo, d), lambda i: (jnp.maximum(tile(i) * halo_blocks - 1, 0), 0)),
            pl.BlockSpec((nbuf_rows, gdim), lambda i: (0, jnp.maximum(i - np_tiles, 0))),
            pl.BlockSpec((1, d), lambda i: (0, 0)),
            pl.BlockSpec((ng, gdim, gdim), lambda i: (0, 0, 0), pipeline_mode=once),
            pl.BlockSpec((1, d), lambda i: (0, 0)),
        ],
        out_specs=[pl.BlockSpec((TM, d), lambda i: (tile(i), 0)),
                   pl.BlockSpec((nhalo, d), lambda i: (jnp.minimum(i, np_tiles - 1), 0)),
                   pl.BlockSpec((TM, d), lambda i: (0, 0), pipeline_mode=once)],
        out_shape=[jax.ShapeDtypeStruct((m, d), F32), jax.ShapeDtypeStruct((np_tiles * nhalo, d), F32),
                   jax.ShapeDtypeStruct((TM, d), F32)],
        scratch_shapes=[pltpu.VMEM((TM + nhalo, 1), F32)],
        compiler_params=pltpu.CompilerParams(
            dimension_semantics=("arbitrary",),
            vmem_limit_bytes=_vmem_limit(vm)),
        name="pool_mixer",
    )(x, x, buf_t, gain.reshape(1, d), w_group, scale.reshape(1, d))


def _ffn_kernel(ce_ref, nu_ref, nv_ref, xb_ref, x_ref, gain_ref, wg_hbm, wu_hbm, wd_hbm, o_ref,
                wg_buf, wu_buf, wd_buf, sems, maybe_xb_scr=None, *, dense, nf):
    del xb_ref
    c = pl.program_id(0)
    nch = pl.num_programs(0)
    rows = x_ref.shape[0]
    nvalid = nv_ref[c]
    units = nu_ref[c]

    def weight_copies(chunk, f, slot):
        e = ce_ref[chunk]
        cols = pl.ds(pl.multiple_of(f * FFN_TF, FFN_TF), FFN_TF)
        return (pltpu.make_async_copy(wg_hbm.at[e, :, cols], wg_buf.at[slot], sems.at[0, slot]),
                pltpu.make_async_copy(wu_hbm.at[e, :, cols], wu_buf.at[slot], sems.at[1, slot]),
                pltpu.make_async_copy(wd_hbm.at[e, cols, :], wd_buf.at[slot], sems.at[2, slot]))

    def start_weights(chunk, f, slot):
        for copy in weight_copies(chunk, f, slot):
            copy.start()

    def wait_weights(chunk, f, slot):
        for copy in weight_copies(chunk, f, slot):
            copy.wait()

    @pl.when(jnp.logical_and(c == 0, units > 0))
    def _prime():
        start_weights(0, 0, 0)

    for s in range(rows // FFN_UNIT):
        rs = slice(s * FFN_UNIT, (s + 1) * FFN_UNIT)
        if dense:
            x = x_ref[rs, :]
            live = (s * FFN_UNIT + lax.broadcasted_iota(jnp.int32, (FFN_UNIT, 1), 0)) < nvalid
            o_ref[rs, :] = x
            maybe_xb_scr[rs, :] = jnp.where(live, _rms(x, gain_ref[...]), 0.0).astype(BF16)
        else:
            o_ref[rs, :] = jnp.zeros((FFN_UNIT, o_ref.shape[1]), F32)

    def sub_tile(slot, start, nrows):
        rs = pl.ds(pl.multiple_of(start, FFN_UNIT), nrows)
        xs = maybe_xb_scr[rs, :] if dense else x_ref[rs, :].astype(BF16)
        gate = _bdot(xs, wg_buf[slot].astype(BF16))
        up = _bdot(xs, wu_buf[slot].astype(BF16))
        act = (gate * jax.nn.sigmoid(gate) * up).astype(BF16)
        o_ref[rs, :] += _bdot(act, wd_buf[slot].astype(BF16))

    def swiglu_block(slot):
        full = rows // FFN_UNIT
        straight = tuple(range(full - FFN_STRAIGHT + 1, full + 1))
        for n in straight:
            @pl.when(units == n)
            def _straight(n=n):
                sub_tile(slot, 0, n * FFN_UNIT)

        @pl.when(units < straight[0])
        def _partial():
            ngroup = lax.shift_right_logical(units, FFN_GROUP.bit_length() - 1)

            def group(s, carry):
                sub_tile(slot, s * (FFN_GROUP * FFN_UNIT), FFN_GROUP * FFN_UNIT)
                return carry

            lax.fori_loop(0, ngroup, group, 0)
            done = ngroup * FFN_GROUP
            for width in [FFN_GROUP >> k for k in range(1, FFN_GROUP.bit_length())]:
                bit = lax.bitwise_and(units, width)

                @pl.when(bit != 0)
                def _rest(width=width, done=done):
                    sub_tile(slot, done * FFN_UNIT, width * FFN_UNIT)

                done = done + bit

    @pl.when(units > 0)
    def _work():
        nxt = jnp.minimum(c + 1, nch - 1)
        next_live = jnp.logical_and(c + 1 < nch, nu_ref[nxt] > 0)

        def block(f, carry):
            slot = lax.bitwise_and(f, 1)
            wait_weights(c, f, slot)

            @pl.when(f + 1 < nf)
            def _():
                start_weights(c, f + 1, 1 - slot)

            @pl.when(jnp.logical_and(f + 1 == nf, next_live))
            def _():
                start_weights(nxt, 0, 1 - slot)

            swiglu_block(slot)
            return carry

        lax.fori_loop(0, nf, block, 0)


def _ffn(x, gain, wg, wu, wd, chunk_expert, chunk_units, chunk_nvalid, *, dense):
    m, d = x.shape
    dff = wg.shape[-1]
    nch = chunk_expert.shape[0]
    nf = dff // FFN_TF
    assert dff % FFN_TF == 0 and nf % 2 == 0
    last_live = jnp.maximum(jnp.sum(chunk_units > 0) - 1, 0).astype(jnp.int32)
    x_block = jnp.minimum(jnp.arange(nch, dtype=jnp.int32), last_live)
    vm = (2 * FFN_ROWS * d * 4 + FFN_ROWS * d * 2 + 3 * 2 * d * FFN_TF * 4
          + 3 * d * FFN_TF * 2 + 16 * FFN_UNIT * FFN_TF * 4)
    once = pl.Buffered(1)
    grid_spec = pltpu.PrefetchScalarGridSpec(
        num_scalar_prefetch=4,
        grid=(nch,),
        in_specs=[
            pl.BlockSpec((FFN_ROWS, d), lambda c, ce, nu, nv, xb: (xb[c], 0), pipeline_mode=once),
            pl.BlockSpec((1, d), lambda c, ce, nu, nv, xb: (0, 0)),
            pl.BlockSpec(memory_space=pl.ANY),
            pl.BlockSpec(memory_space=pl.ANY),
            pl.BlockSpec(memory_space=pl.ANY),
        ],
        out_specs=pl.BlockSpec((FFN_ROWS, d), lambda c, ce, nu, nv, xb: (c, 0),
                               pipeline_mode=once if dense else None),
        scratch_shapes=[pltpu.VMEM((2, d, FFN_TF), F32), pltpu.VMEM((2, d, FFN_TF), F32),
                        pltpu.VMEM((2, FFN_TF, d), F32), pltpu.SemaphoreType.DMA((3, 2))]
        + ([pltpu.VMEM((FFN_ROWS, d), BF16)] if dense else []),
    )
    return pl.pallas_call(
        functools.partial(_ffn_kernel, dense=dense, nf=nf),
        grid_spec=grid_spec,
        out_shape=jax.ShapeDtypeStruct((m, d), F32),
        compiler_params=pltpu.CompilerParams(
            dimension_semantics=("arbitrary",),
            vmem_limit_bytes=_vmem_limit(vm)),
        name="ffn_dense" if dense else "ffn_experts",
    )(chunk_expert, chunk_units, chunk_nvalid, x_block, x, gain.reshape(1, d), wg, wu, wd)


def _dense_ffn(x, gain, wg, wu, wd, layer):
    m = x.shape[0]
    nch = pl.cdiv(m, FFN_ROWS)
    nvalid = jnp.minimum(FFN_ROWS, m - FFN_ROWS * jnp.arange(nch, dtype=jnp.int32))
    units = (nvalid + FFN_UNIT - 1) // FFN_UNIT
    which = jnp.full((nch,), layer, jnp.int32)
    return _ffn(x, gain, wg, wu, wd, which, units, nvalid, dense=True)


def _route_kernel(x_ref, g_ref, r_ref, hn_ref, info_ref, fields_ref, cnt_ref, carry_scr, *, n_experts):
    i = pl.program_id(0)
    tm = x_ref.shape[0]

    @pl.when(i == 0)
    def _():
        carry_scr[...] = jnp.zeros_like(carry_scr)

    hn = _rms(x_ref[...], g_ref[...])
    hn_ref[...] = hn
    lane = lax.broadcasted_iota(jnp.int32, (tm, LANES), 1).astype(F32)
    logits = jnp.where(lane < n_experts, _split_dot(hn, r_ref[...]), -jnp.inf)
    m1 = jnp.max(logits, axis=-1, keepdims=True)
    i1 = jnp.min(jnp.where(logits == m1, lane, float(LANES)), axis=-1, keepdims=True)
    rest = jnp.where(lane == i1, -jnp.inf, logits)
    m2 = jnp.max(rest, axis=-1, keepdims=True)
    i2 = jnp.min(jnp.where(rest == m2, lane, float(LANES)), axis=-1, keepdims=True)
    e = jnp.exp(m2 - m1)
    g1 = 1.0 / (1.0 + e)
    g2 = e / (1.0 + e)
    oh1 = (lane == i1).astype(F32)
    oh2 = (lane == i2).astype(F32)
    cnt = oh1 + oh2
    row = lax.broadcasted_iota(jnp.int32, (tm, tm), 0)
    col = lax.broadcasted_iota(jnp.int32, (tm, tm), 1)
    before = (col < row).astype(BF16)
    ranks = _bdot(before, cnt.astype(BF16)) + carry_scr[0:1, :]
    rank1 = jnp.sum(ranks * oh1, axis=-1, keepdims=True)
    rank2 = jnp.sum(ranks * oh2, axis=-1, keepdims=True)
    carry_scr[0:1, :] = carry_scr[0:1, :] + jnp.sum(cnt, axis=0, keepdims=True)
    cnt_ref[...] = jnp.broadcast_to(carry_scr[0:1, :], cnt_ref.shape)
    info = jnp.where(lane == 0, i1, 0.0)
    info = jnp.where(lane == 1, i2, info)
    info = jnp.where(lane == 2, g1, info)
    info = jnp.where(lane == 3, g2, info)
    info = jnp.where(lane == 4, rank1, info)
    info = jnp.where(lane == 5, rank2, info)
    info_ref[...] = info
    fields_ref[...] = info.T[:fields_ref.shape[0]]


def _route(x, gain, router):
    m, d = x.shape
    ne = router.shape[1]
    rpad = jnp.pad(router, ((0, 0), (0, LANES - ne)))
    vm = 4 * TM * d * 4 + 2 * d * LANES * 4 + 4 * TM * LANES * 4 + TM * TM * 8 + 8 * TM * d * 4
    return pl.pallas_call(
        functools.partial(_route_kernel, n_experts=ne),
        grid=(m // TM,),
        in_specs=[
            pl.BlockSpec((TM, d), lambda i: (i, 0)),
            pl.BlockSpec((1, d), lambda i: (0, 0)),
            pl.BlockSpec((d, LANES), lambda i: (0, 0)),
        ],
        out_specs=[pl.BlockSpec((TM, d), lambda i: (i, 0)),
                   pl.BlockSpec((TM, LANES), lambda i: (i, 0)),
                   pl.BlockSpec((SUBLANES, TM), lambda i: (0, i)),
                   pl.BlockSpec((SUBLANES, LANES), lambda i: (0, 0))],
        out_shape=[jax.ShapeDtypeStruct((m, d), F32), jax.ShapeDtypeStruct((m, LANES), F32),
                   jax.ShapeDtypeStruct((SUBLANES, m), F32), jax.ShapeDtypeStruct((SUBLANES, LANES), F32)],
        scratch_shapes=[pltpu.VMEM((SUBLANES, LANES), F32)],
        compiler_params=pltpu.CompilerParams(
            dimension_semantics=("arbitrary",), vmem_limit_bytes=_vmem_limit(vm)),
        name="route",
    )(x, gain.reshape(1, d), rpad)


def _row_copy(src, src_row, dst, dst_row, sem):
    return pltpu.make_async_copy(src.at[pl.ds(src_row, 1), :], dst.at[pl.ds(dst_row, 1), :], sem)


def _dispatch_kernel(p1_ref, p2_ref, hn_ref, init_ref, xs_ref, sem):
    del init_ref
    base = pl.program_id(0) * hn_ref.shape[0]
    tm = hn_ref.shape[0]

    def start(r, carry):
        _row_copy(hn_ref, r, xs_ref, p1_ref[base + r], sem).start()
        _row_copy(hn_ref, r, xs_ref, p2_ref[base + r], sem).start()
        return carry

    def wait(r, carry):
        _row_copy(hn_ref, r, xs_ref, p1_ref[base + r], sem).wait()
        _row_copy(hn_ref, r, xs_ref, p2_ref[base + r], sem).wait()
        return carry

    lax.fori_loop(0, tm, start, 0, unroll=ROW_DMA_UNROLL)
    lax.fori_loop(0, tm, wait, 0, unroll=ROW_DMA_UNROLL)


def _dispatch(hn, pos1, pos2, init):
    m, d = hn.shape
    n_rows = init.shape[0]
    assert init.shape == (n_rows, d) and init.dtype == F32
    grid_spec = pltpu.PrefetchScalarGridSpec(
        num_scalar_prefetch=2,
        grid=(m // ROUTE_TM,),
        in_specs=[pl.BlockSpec((ROUTE_TM, d), lambda i, p1, p2: (i, 0)),
                  pl.BlockSpec(memory_space=pl.ANY)],
        out_specs=pl.BlockSpec(memory_space=pl.ANY),
        scratch_shapes=[pltpu.SemaphoreType.DMA(())],
    )
    return pl.pallas_call(
        _dispatch_kernel,
        grid_spec=grid_spec,
        out_shape=jax.ShapeDtypeStruct((n_rows, d), F32),
        input_output_aliases={3: 0},
        compiler_params=pltpu.CompilerParams(dimension_semantics=("arbitrary",)),
        name="dispatch",
    )(pos1, pos2, hn, init)


def _combine_kernel(p1_ref, p2_ref, x_ref, info_ref, gain_ref, ys_ref, *refs, final_norm, head_tiles):
    *out_refs, b1_scr, b2_scr, sems = refs
    tm = x_ref.shape[0]
    i = pl.program_id(0)
    slot = lax.bitwise_and(i, 1)

    def gather(tile, sl, go):
        def row(r, carry):
            t = tile * tm + r
            for pos_ref, buf in ((p1_ref, b1_scr), (p2_ref, b2_scr)):
                copy = _row_copy(ys_ref, pos_ref[t], buf.at[sl], r, sems.at[sl])
                copy.start() if go else copy.wait()
            return carry

        lax.fori_loop(0, tm, row, 0, unroll=ROW_DMA_UNROLL)

    @pl.when(i == 0)
    def _():
        gather(0, 0, True)

    @pl.when(i + 1 < pl.num_programs(0))
    def _():
        gather(i + 1, 1 - slot, True)

    gather(i, slot, False)
    info = info_ref[...]
    e1, e2 = info[:, 0:1], info[:, 1:2]
    g1, g2 = info[:, 2:3], info[:, 3:4]
    y1 = g1 * b1_scr[slot]
    y2 = g2 * b2_scr[slot]
    lo = jnp.where(e1 < e2, y1, y2)
    hi = jnp.where(e1 < e2, y2, y1)
    out = x_ref[...] + (lo + hi)
    if final_norm:
        out = _rms(out, gain_ref[...])
    if head_tiles is None:
        out_refs[0][...] = out
    else:
        head_ref, tail_ref = out_refs

        @pl.when(i < head_tiles)
        def _():
            head_ref[...] = out

        @pl.when(i >= head_tiles)
        def _():
            tail_ref[...] = out


def _combine(x, info, ys, pos1, pos2, gain, *, final_norm, head_rows=None):
    m, d = x.shape
    vm = 6 * ROUTE_TM * d * 4 + 2 * ROUTE_TM * LANES * 4 + 4 * ROUTE_TM * d * 4 + 6 * ROUTE_TM * d * 4
    if head_rows is None:
        head_tiles = None
        out_specs = pl.BlockSpec((ROUTE_TM, d), lambda i, p1, p2: (i, 0))
        out_shape = jax.ShapeDtypeStruct((m, d), F32)
    else:
        assert head_rows % ROUTE_TM == 0 and 0 < head_rows < m
        head_tiles = head_rows // ROUTE_TM
        out_specs = [pl.BlockSpec((ROUTE_TM, d), lambda i, p1, p2: (jnp.minimum(i, head_tiles - 1), 0)),
                     pl.BlockSpec((ROUTE_TM, d), lambda i, p1, p2: (jnp.maximum(i - head_tiles, 0), 0))]
        out_shape = [jax.ShapeDtypeStruct((head_rows, d), F32), jax.ShapeDtypeStruct((m - head_rows, d), F32)]
    grid_spec = pltpu.PrefetchScalarGridSpec(
        num_scalar_prefetch=2,
        grid=(m // ROUTE_TM,),
        in_specs=[
            pl.BlockSpec((ROUTE_TM, d), lambda i, p1, p2: (i, 0)),
            pl.BlockSpec((ROUTE_TM, LANES), lambda i, p1, p2: (i, 0)),
            pl.BlockSpec((1, d), lambda i, p1, p2: (0, 0)),
            pl.BlockSpec(memory_space=pl.ANY),
        ],
        out_specs=out_specs,
        scratch_shapes=[pltpu.VMEM((2, ROUTE_TM, d), F32), pltpu.VMEM((2, ROUTE_TM, d), F32),
                        pltpu.SemaphoreType.DMA((2,))],
    )
    return pl.pallas_call(
        functools.partial(_combine_kernel, final_norm=final_norm, head_tiles=head_tiles),
        grid_spec=grid_spec,
        out_shape=out_shape,
        compiler_params=pltpu.CompilerParams(
            dimension_semantics=("arbitrary",), vmem_limit_bytes=_vmem_limit(vm)),
        name="combine_final" if final_norm else "combine",
    )(pos1, pos2, x, info, gain.reshape(1, d), ys)


def _moe_ffn(x, gain, router, wg, wu, wd, layer, out_gain, *, final_norm, head_rows=None, spare=None):
    m, d = x.shape
    ne = router.shape[1]
    wg, wu, wd = (w.reshape((-1,) + w.shape[2:]) for w in (wg, wu, wd))
    assert m % TM == 0 and m % ROUTE_TM == 0
    hn, info, fields, cnt = _route(x, gain, router)
    fields = fields.astype(jnp.int32)
    counts = cnt[0, :ne].astype(jnp.int32)
    chunks_per = (counts + FFN_ROWS - 1) // FFN_ROWS
    chunk_end = jnp.cumsum(chunks_per)
    chunk_start = chunk_end - chunks_per
    share = (counts + jnp.maximum(chunks_per, 1) - 1) // jnp.maximum(chunks_per, 1)
    share = jnp.maximum((share + FFN_UNIT - 1) // FFN_UNIT * FFN_UNIT, FFN_UNIT)

    def position(e, rank):
        j = rank // share[e]
        return (chunk_start[e] + j) * FFN_ROWS + rank - j * share[e]

    pos1 = position(fields[0], fields[4])
    pos2 = position(fields[1], fields[5])
    nch = (m * TOP_K) // FFN_ROWS + ne
    cidx = jnp.arange(nch, dtype=jnp.int32)
    last = jnp.maximum(chunk_end[-1] - 1, 0)
    owner = jnp.sum(jnp.minimum(cidx, last)[:, None] >= chunk_end[None, :], axis=1).astype(jnp.int32)
    owner = jnp.minimum(owner, ne - 1)
    local = cidx - chunk_start[owner]
    nvalid = jnp.clip(counts[owner] - local * share[owner], 0, share[owner])
    nvalid = jnp.where(cidx < chunk_end[-1], nvalid, 0).astype(jnp.int32)
    units = (nvalid + FFN_UNIT - 1) // FFN_UNIT
    if spare is None:
        spare = jnp.zeros((nch * FFN_ROWS, d), F32)
    xs = _dispatch(hn, pos1, pos2, spare)
    ys = _ffn(xs, gain, wg, wu, wd, owner + layer * ne, units, nvalid, dense=False)
    out = _combine(x, info, ys, pos1, pos2, out_gain, final_norm=final_norm, head_rows=head_rows)
    return out, ys


def kernel(x_prompt, x_sample, state_conv, state_pool, norm_mix, norm_ffn, final_norm, a_w_in, a_v_gain, a_w_s, a_b_s, a_w_out, b_w_in, b_conv, b_w_out, c_w_group, c_scale, ffn_w_gate, ffn_w_up, ffn_w_down, moe_router, moe_w_gate, moe_w_up, moe_w_down):
    batch, seq, d = x_prompt.shape
    slab, steps, _ = x_sample.shape
    depth = norm_mix.shape[0]
    assert depth % 2 == 0, "the final RMSNorm is fused into the routed combine kernel of the last layer"
    m_p = batch * seq
    geo = dict(m_p=m_p, slab=slab, steps=steps)

    def to_time_major(a):
        return a.transpose(1, 0, 2).reshape(a.shape[1] * slab, a.shape[2])

    def from_time_major(a, r):
        return a.reshape(r, slab, a.shape[-1]).transpose(1, 0, 2)

    def seq_tails(a, r):
        return jnp.stack([a[(b + 1) * seq - r:(b + 1) * seq] for b in range(batch)])

    a_w_out_b = a_w_out.astype(BF16)
    b_w_out_b = b_w_out.astype(BF16)
    x = jnp.concatenate([x_prompt.reshape(m_p, d), to_time_major(x_sample)], axis=0)
    chunk_v_s, conv_p, conv_s, pool_p, pool_s = [], [], [], [], []
    spare = None
    for i in range(depth):
        j, kind = divmod(i, 3)
        if kind == 0:
            z = _inproj_gelu(x, norm_mix[i], a_w_in, j)
            x, cv = _chunk_out(z, x, a_v_gain[j], a_w_s[j], a_b_s[j], a_w_out_b, j, **geo)
            chunk_v_s.append(from_time_major(cv, steps))
        elif kind == 1:
            hist = b_conv.shape[1] - 1
            bg, zc = _inproj_conv(x, norm_mix[i], b_w_in, j)
            x = _conv_out(bg, zc, x, to_time_major(state_conv[j]), b_conv[j], b_w_out_b, j, seq=seq, **geo)
            conv_p.append(seq_tails(zc, hist))
            conv_s.append(from_time_major(zc[m_p:], steps)[:, steps - hist:])
        else:
            nbuf = state_pool.shape[2]
            x, tails, hn_s = _pool_mixer(x, norm_mix[i], to_time_major(state_pool[j]), c_w_group[j],
                                         c_scale[j], seq=seq, **geo)
            per_tile = tails.shape[0] // (m_p // TM)
            ends = tails.reshape(batch, seq // TM, per_tile, d)[:, -1]
            pool_p.append(ends[:, per_tile - nbuf:])
            hc = jnp.concatenate([state_pool[j], from_time_major(hn_s, steps)], axis=1)
            pool_s.append(hc[:, hc.shape[1] - nbuf:])
        f = i // 2
        if i % 2 == 0:
            x = _dense_ffn(x, norm_ffn[i], ffn_w_gate, ffn_w_up, ffn_w_down, f)
        else:
            last = i == depth - 1
            x, spare = _moe_ffn(x, norm_ffn[i], moe_router[f], moe_w_gate, moe_w_up, moe_w_down, f,
                                final_norm, final_norm=last, head_rows=m_p if last else None, spare=spare)
    y_prompt = x[0].reshape(batch, seq, d)
    y_sample = from_time_major(x[1], steps)
    return (y_prompt, y_sample, jnp.stack(chunk_v_s), jnp.stack(conv_p), jnp.stack(conv_s),
            jnp.stack(pool_p), jnp.stack(pool_s))
```

```python
import functools
import math

import jax
import jax.numpy as jnp
from jax import lax
from jax.experimental import pallas as pl
from jax.experimental.pallas import tpu as pltpu

F32 = jnp.float32
BF16 = jnp.bfloat16
EPS = 1e-6
PAST_LEN = 16384
POOL_WINDOWS = (2, 4, 8, 16)
TOP_K = 2

V7X_VMEM_BYTES = 64 * 1024 * 1024
SUBLANES = 8
LANES = 128

TM = 512
TN = 512
FFN_UNIT = 64
FFN_ROWS = 18 * FFN_UNIT
FFN_STRAIGHT = 3
FFN_GROUP = 8
FFN_TF = 512
DISPATCH_TM = 512
ROUTE_TM = 256
ROW_DMA_UNROLL = 32


def _vmem_limit(nbytes):
    return min(int(nbytes) + (6 << 20), V7X_VMEM_BYTES - (4 << 20))


def _rms(x, gain):
    ms = jnp.mean(x * x, axis=-1, keepdims=True)
    return x * lax.rsqrt(ms + EPS) * gain


def _gelu_tanh(x):
    c = math.sqrt(2.0 / math.pi)
    return 0.5 * x * (1.0 + jnp.tanh(c * (x + 0.044715 * (x * x * x))))


def _bdot(a, b):
    return jnp.dot(a, b, preferred_element_type=F32)


def _split_dot(a, b):
    ah = a.astype(BF16)
    al = (a - ah.astype(F32)).astype(BF16)
    bh = b.astype(BF16)
    bl = (b - bh.astype(F32)).astype(BF16)
    return _bdot(ah, bh) + _bdot(ah, bl) + _bdot(al, bh)


INPROJ_ROW_TILES = 8


def _inproj_rows(m):
    assert m % (INPROJ_ROW_TILES * 2 * SUBLANES) == 0
    return m // INPROJ_ROW_TILES


def _inproj_gelu_kernel(x_ref, g_ref, w_ref, o_ref, h_scr):
    @pl.when(pl.program_id(1) == 0)
    def _():
        h_scr[...] = _rms(x_ref[...], g_ref[...]).astype(BF16)

    z = _bdot(h_scr[...], w_ref[0].astype(BF16))
    o_ref[...] = _gelu_tanh(z)


def _inproj_gelu(x, gain, w, layer):
    m, d = x.shape
    n = w.shape[2]
    tm = _inproj_rows(m)
    tn = 2 * TN
    vm = 2 * tm * d * 4 + tm * d * 2 + 2 * d * tn * 4 + d * tn * 2 + 4 * tm * tn * 4
    return pl.pallas_call(
        _inproj_gelu_kernel,
        grid=(m // tm, n // tn),
        in_specs=[
            pl.BlockSpec((tm, d), lambda i, j: (i, 0)),
            pl.BlockSpec((1, d), lambda i, j: (0, 0)),
            pl.BlockSpec((1, d, tn), lambda i, j: (layer, 0, j)),
        ],
        out_specs=pl.BlockSpec((tm, tn), lambda i, j: (i, j)),
        out_shape=jax.ShapeDtypeStruct((m, n), F32),
        scratch_shapes=[pltpu.VMEM((tm, d), BF16)],
        compiler_params=pltpu.CompilerParams(
            dimension_semantics=("arbitrary", "arbitrary"),
            vmem_limit_bytes=_vmem_limit(vm)),
        name="inproj_gelu",
    )(x, gain.reshape(1, d), w)


def _inproj_conv_kernel(x_ref, g_ref, wb_ref, wc_ref, wh_ref, ob_ref, oz_ref, h_scr):
    @pl.when(pl.program_id(1) == 0)
    def _():
        h_scr[...] = _rms(x_ref[...], g_ref[...]).astype(BF16)

    h = h_scr[...]
    ob_ref[...] = _bdot(h, wb_ref[0].astype(BF16))
    c = _bdot(h, wc_ref[0].astype(BF16))
    hin = _bdot(h, wh_ref[0].astype(BF16))
    oz_ref[...] = c * hin


def _inproj_conv(x, gain, w, layer):
    m, d = x.shape
    cd = w.shape[2] // 3
    tm = _inproj_rows(m)
    tn = TN // 2
    nb = cd // tn
    vm = 2 * tm * d * 4 + tm * d * 2 + 3 * (2 * d * tn * 4 + d * tn * 2) + 10 * tm * tn * 4
    out = jax.ShapeDtypeStruct((m, cd), F32)
    return pl.pallas_call(
        _inproj_conv_kernel,
        grid=(m // tm, nb),
        in_specs=[
            pl.BlockSpec((tm, d), lambda i, j: (i, 0)),
            pl.BlockSpec((1, d), lambda i, j: (0, 0)),
            pl.BlockSpec((1, d, tn), lambda i, j: (layer, 0, j)),
            pl.BlockSpec((1, d, tn), lambda i, j: (layer, 0, j + nb)),
            pl.BlockSpec((1, d, tn), lambda i, j: (layer, 0, j + 2 * nb)),
        ],
        out_specs=[pl.BlockSpec((tm, tn), lambda i, j: (i, j)),
                   pl.BlockSpec((tm, tn), lambda i, j: (i, j))],
        out_shape=[out, out],
        scratch_shapes=[pltpu.VMEM((tm, d), BF16)],
        compiler_params=pltpu.CompilerParams(
            dimension_semantics=("arbitrary", "arbitrary"),
            vmem_limit_bytes=_vmem_limit(vm)),
        name="inproj_conv",
    )(x, gain.reshape(1, d), w, w, w)


def _chunk_out_kernel(u_ref, v_ref, vg_ref, gw_ref, gb_ref, sw_ref, sb_ref, w_ref, x_ref,
                      o_ref, cv_ref, p_scr, *, np_tiles, chunk, groups, slab, steps):
    i = pl.program_id(0)
    gd = p_scr.shape[1] // groups

    @pl.when(i < np_tiles)
    def _prompt():
        row = lax.broadcasted_iota(jnp.int32, (chunk, chunk), 0)
        col = lax.broadcasted_iota(jnp.int32, (chunk, chunk), 1)
        causal = col <= row
        for s in range(p_scr.shape[0] // chunk):
            rs = slice(s * chunk, (s + 1) * chunk)
            vb = _rms(v_ref[rs, :], vg_ref[...]).astype(BF16)
            for h in range(groups):
                cs = slice(h * gd, (h + 1) * gd)
                wm = jnp.where(causal, gw_ref[h], 0.0).astype(BF16)
                mixed = _bdot(wm, vb[:, cs]) + gb_ref[:, h:h + 1]
                p_scr[rs, cs] = (u_ref[rs, cs] * mixed).astype(BF16)

    @pl.when(i >= np_tiles)
    def _sample():
        for t in range(steps):
            rs = slice(t * slab, (t + 1) * slab)
            cv_ref[rs, :] = _rms(v_ref[rs, :], vg_ref[...])
        for t in range(steps):
            rs = slice(t * slab, (t + 1) * slab)
            for h in range(groups):
                cs = slice(h * gd, (h + 1) * gd)
                mixed = sw_ref[t * steps:t * steps + 1, cs] * cv_ref[0:slab, cs]
                for s in range(1, t + 1):
                    k = t * steps + s
                    mixed = mixed + sw_ref[k:k + 1, cs] * cv_ref[s * slab:(s + 1) * slab, cs]
                mixed = mixed + sb_ref[t:t + 1, cs]
                p_scr[rs, cs] = (u_ref[rs, cs] * mixed).astype(BF16)

    o_ref[...] = x_ref[...] + _bdot(p_scr[...], w_ref[0])


def _chunk_out(z, x, v_gain, w_s, b_s, w_out, layer, *, m_p, slab, steps):
    m, d = x.shape
    a = z.shape[1] // 2
    groups, chunk = w_s.shape[0], w_s.shape[1]
    gd = a // groups
    assert m_p % TM == 0 and TM % chunk == 0 and m - m_p == TM == slab * steps
    assert steps <= chunk and slab % SUBLANES == 0 and w_out.dtype == BF16
    np_tiles = m_p // TM
    gb = b_s.T
    sw = jnp.repeat(w_s[:, :steps, :steps].transpose(1, 2, 0).reshape(steps * steps, groups), gd, axis=1)
    sb = jnp.repeat(b_s[:, :steps].T, gd, axis=1)
    vm = (4 * TM * a * 4 + TM * a * 2 + 2 * groups * chunk * chunk * 4 + a * d * 2
          + 4 * TM * d * 4 + TM * a * 4 + 4 * (steps * steps + steps) * a * 4 + 2 * TM * d * 4)
    once = pl.Buffered(1)
    kern = functools.partial(_chunk_out_kernel, np_tiles=np_tiles, chunk=chunk, groups=groups,
                             slab=slab, steps=steps)
    return pl.pallas_call(
        kern,
        grid=(m // TM,),
        in_specs=[
            pl.BlockSpec((TM, a), lambda i: (i, 0)),
            pl.BlockSpec((TM, a), lambda i: (i, 1)),
            pl.BlockSpec((1, a), lambda i: (0, 0)),
            pl.BlockSpec((groups, chunk, chunk), lambda i: (0, 0, 0)),
            pl.BlockSpec((chunk, groups), lambda i: (0, 0)),
            pl.BlockSpec((steps * steps, a), lambda i: (0, 0)),
            pl.BlockSpec((steps, a), lambda i: (0, 0)),
            pl.BlockSpec((1, a, d), lambda i: (layer, 0, 0), pipeline_mode=once),
            pl.BlockSpec((TM, d), lambda i: (i, 0)),
        ],
        out_specs=[pl.BlockSpec((TM, d), lambda i: (i, 0)),
                   pl.BlockSpec((TM, a), lambda i: (0, 0), pipeline_mode=once)],
        out_shape=[jax.ShapeDtypeStruct((m, d), F32), jax.ShapeDtypeStruct((TM, a), F32)],
        scratch_shapes=[pltpu.VMEM((TM, a), BF16)],
        compiler_params=pltpu.CompilerParams(
            dimension_semantics=("arbitrary",),
            vmem_limit_bytes=_vmem_limit(vm)),
        name="chunk_out",
    )(z, z, v_gain.reshape(1, a), w_s, gb, sw, sb, w_out, x)


def _conv_out_kernel(bg_ref, z_ref, halo_ref, st_ref, cw_ref, w_ref, x_ref, o_ref, p_scr,
                     *, np_tiles, tiles_per_seq, slab, steps):
    i = pl.program_id(0)
    width = cw_ref.shape[0]
    nhalo = halo_ref.shape[0]
    tm, cdim = p_scr.shape

    @pl.when(i < np_tiles)
    def _prompt():
        keep = (i % tiles_per_seq != 0).astype(F32)
        for c in range(cdim // TN):
            cs = slice(c * TN, (c + 1) * TN)
            ext = jnp.concatenate([halo_ref[:, cs] * keep, z_ref[:, cs]], axis=0)
            conv = cw_ref[0:1, cs] * pltpu.roll(ext, width - 1, axis=0)[nhalo:]
            for k in range(1, width):
                sh = width - 1 - k
                zk = pltpu.roll(ext, sh, axis=0)[nhalo:] if sh else z_ref[:, cs]
                conv = conv + cw_ref[k:k + 1, cs] * zk
            p_scr[:, cs] = (bg_ref[:, cs] * conv).astype(BF16)

    @pl.when(i >= np_tiles)
    def _sample():
        hist = width - 1

        def zrow(r, cs):
            if r < hist:
                return st_ref[r * slab:(r + 1) * slab, cs]
            return z_ref[(r - hist) * slab:(r - hist + 1) * slab, cs]

        for c in range(cdim // TN):
            cs = slice(c * TN, (c + 1) * TN)
            for t in range(steps):
                conv = cw_ref[0:1, cs] * zrow(t, cs)
                for k in range(1, width):
                    conv = conv + cw_ref[k:k + 1, cs] * zrow(t + k, cs)
                rs = slice(t * slab, (t + 1) * slab)
                p_scr[rs, cs] = (bg_ref[rs, cs] * conv).astype(BF16)

    o_ref[...] = x_ref[...] + _bdot(p_scr[...], w_ref[0])


def _conv_out(bg, zc, x, state_t, conv_w, w_out, layer, *, m_p, seq, slab, steps):
    m, d = x.shape
    cd = zc.shape[1]
    width = conv_w.shape[0]
    assert m - m_p == TM == slab * steps and seq % TM == 0 and width - 1 <= SUBLANES
    assert w_out.dtype == BF16
    np_tiles = m_p // TM
    halo_blocks = TM // SUBLANES
    vm = (4 * TM * cd * 4 + 2 * SUBLANES * cd * 4 + (width - 1) * slab * cd * 4 + TM * cd * 2
          + cd * d * 2 + 4 * TM * d * 4 + 2 * TM * d * 4 + 6 * TM * TN * 4)
    kern = functools.partial(_conv_out_kernel, np_tiles=np_tiles, tiles_per_seq=seq // TM,
                             slab=slab, steps=steps)
    once = pl.Buffered(1)
    return pl.pallas_call(
        kern,
        grid=(m // TM,),
        in_specs=[
            pl.BlockSpec((TM, cd), lambda i: (i, 0)),
            pl.BlockSpec((TM, cd), lambda i: (i, 0)),
            pl.BlockSpec((SUBLANES, cd), lambda i: (jnp.maximum(i * halo_blocks - 1, 0), 0)),
            pl.BlockSpec(((width - 1) * slab, cd), lambda i: (0, 0), pipeline_mode=once),
            pl.BlockSpec((width, cd), lambda i: (0, 0)),
            pl.BlockSpec((1, cd, d), lambda i: (layer, 0, 0), pipeline_mode=once),
            pl.BlockSpec((TM, d), lambda i: (i, 0)),
        ],
        out_specs=pl.BlockSpec((TM, d), lambda i: (i, 0)),
        out_shape=jax.ShapeDtypeStruct((m, d), F32),
        scratch_shapes=[pltpu.VMEM((TM, cd), BF16)],
        compiler_params=pltpu.CompilerParams(
            dimension_semantics=("arbitrary",),
            vmem_limit_bytes=_vmem_limit(vm)),
        name="conv_out",
    )(bg, zc, zc, state_t, conv_w, w_out, x)


def _pool_kernel(x_ref, halo_ref, buf_ref, gain_ref, w_ref, sc_ref, o_ref, tail_ref, hs_ref, r_scr,
                 *, np_tiles, tiles_per_seq, slab, steps, windows, pos0):
    i = pl.program_id(0)
    tm = x_ref.shape[0]
    nhalo = halo_ref.shape[0]
    nbuf = buf_ref.shape[0] // slab
    gdim = buf_ref.shape[1]

    @pl.when(i <= np_tiles)
    def _():
        x = x_ref[...]
        r_scr[nhalo:, :] = lax.rsqrt(jnp.mean(x * x, axis=-1, keepdims=True) + EPS)
        hx = halo_ref[...]
        r_scr[:nhalo, :] = lax.rsqrt(jnp.mean(hx * hx, axis=-1, keepdims=True) + EPS)

    def normed(cs):
        return x_ref[:, cs] * r_scr[nhalo:, :] * gain_ref[:, cs]

    def finish(gi, cs, h, pooled):
        diff = (pooled - h).astype(BF16)
        y = _bdot(diff, w_ref[gi].astype(BF16)) * sc_ref[:, cs]
        o_ref[:, cs] = x_ref[:, cs] + y

    @pl.when(i < np_tiles)
    def _prompt():
        seq_tile = i % tiles_per_seq
        keep = (seq_tile != 0).astype(F32)
        pos = seq_tile * tm + lax.broadcasted_iota(jnp.int32, (tm, 1), 0)
        for gi, win in enumerate(windows):
            cs = slice(gi * gdim, (gi + 1) * gdim)
            h = normed(cs)
            tail_ref[:, cs] = h[tm - nhalo:]
            hh = halo_ref[:, cs] * r_scr[:nhalo, :] * gain_ref[:, cs] * keep
            s = jnp.concatenate([hh, h], axis=0)
            k = 1
            while k < win:
                s = s + pltpu.roll(s, k, axis=0)
                k *= 2
            count = jnp.minimum(win, pos + 1).astype(F32)
            finish(gi, cs, h, s[nhalo:] / count)

    for gi, win in enumerate(windows):
        @pl.when(i == np_tiles + gi)
        def _sample(gi=gi, win=win):
            cs = slice(gi * gdim, (gi + 1) * gdim)
            h = normed(cs)
            hs_ref[:, cs] = h

            def hrow(r):
                if r < nbuf:
                    return buf_ref[r * slab:(r + 1) * slab, :]
                return h[(r - nbuf) * slab:(r - nbuf + 1) * slab]

            rows = []
            for t in range(steps):
                acc = hrow(nbuf + t - win + 1)
                for r in range(nbuf + t - win + 2, nbuf + t + 1):
                    acc = acc + hrow(r)
                rows.append(acc / float(min(win, pos0 + t + 1)))
            finish(gi, cs, h, jnp.concatenate(rows, axis=0))


def _pool_mixer(x, gain, buf_t, w_group, scale, *, m_p, seq, slab, steps):
    m, d = x.shape
    ng, gdim = w_group.shape[0], w_group.shape[1]
    nhalo = 2 * SUBLANES
    assert ng == len(POOL_WINDOWS) and max(POOL_WINDOWS) <= nhalo and gdim * ng == d
    assert m - m_p == TM == slab * steps and seq % TM == 0
    assert buf_t.shape[0] // slab >= max(POOL_WINDOWS) - 1
    np_tiles = m_p // TM
    halo_blocks = TM // nhalo
    nbuf_rows = buf_t.shape[0]
    vm = (4 * TM * d * 4 + 2 * nhalo * d * 4 + 2 * nbuf_rows * gdim * 4 + ng * gdim * gdim * 4
          + gdim * gdim * 2 + TM * d * 4 + 2 * nhalo * d * 4 + (TM + nhalo) * LANES * 4 + 12 * TM * gdim * 4)
    kern = functools.partial(_pool_kernel, np_tiles=np_tiles, tiles_per_seq=seq // TM, slab=slab,
                             steps=steps, windows=POOL_WINDOWS, pos0=PAST_LEN)
    tile = lambda i: jnp.minimum(i, np_tiles)
    once = pl.Buffered(1)
    return pl.pallas_call(
        kern,
        grid=(np_tiles + ng,),
        in_specs=[
            pl.BlockSpec((TM, d), lambda i: (tile(i), 0)),
            pl.BlockSpec((nhalo, d), lambda i: (jnp.maximum(tile(i) * halo_blocks - 1, 0), 0)),
            pl.BlockSpec((nbuf_rows, gdim), lambda i: (0, jnp.maximum(i - np_tiles, 0))),
            pl.BlockSpec((1, d), lambda i: (0, 0)),
            pl.BlockSpec((ng, gdim, gdim), lambda i: (0, 0, 0), pipeline_mode=once),
            pl.BlockSpec((1, d), lambda i: (0, 0)),
        ],
        out_specs=[pl.BlockSpec((TM, d), lambda i: (tile(i), 0)),
                   pl.BlockSpec((nhalo, d), lambda i: (jnp.minimum(i, np_tiles - 1), 0)),
                   pl.BlockSpec((TM, d), lambda i: (0, 0), pipeline_mode=once)],
        out_shape=[jax.ShapeDtypeStruct((m, d), F32), jax.ShapeDtypeStruct((np_tiles * nhalo, d), F32),
                   jax.ShapeDtypeStruct((TM, d), F32)],
        scratch_shapes=[pltpu.VMEM((TM + nhalo, 1), F32)],
        compiler_params=pltpu.CompilerParams(
            dimension_semantics=("arbitrary",),
            vmem_limit_bytes=_vmem_limit(vm)),
        name="pool_mixer",
    )(x, x, buf_t, gain.reshape(1, d), w_group, scale.reshape(1, d))


def _ffn_kernel(ce_ref, nu_ref, nv_ref, xb_ref, x_ref, gain_ref, wg_hbm, wu_hbm, wd_hbm, o_ref,
                wg_buf, wu_buf, wd_buf, sems, maybe_xb_scr=None, *, dense, nf):
    del xb_ref
    c = pl.program_id(0)
    nch = pl.num_programs(0)
    rows = x_ref.shape[0]
    nvalid = nv_ref[c]
    units = nu_ref[c]

    def weight_copies(chunk, f, slot):
        e = ce_ref[chunk]
        cols = pl.ds(pl.multiple_of(f * FFN_TF, FFN_TF), FFN_TF)
        return (pltpu.make_async_copy(wg_hbm.at[e, :, cols], wg_buf.at[slot], sems.at[0, slot]),
                pltpu.make_async_copy(wu_hbm.at[e, :, cols], wu_buf.at[slot], sems.at[1, slot]),
                pltpu.make_async_copy(wd_hbm.at[e, cols, :], wd_buf.at[slot], sems.at[2, slot]))

    def start_weights(chunk, f, slot):
        for copy in weight_copies(chunk, f, slot):
            copy.start()

    def wait_weights(chunk, f, slot):
        for copy in weight_copies(chunk, f, slot):
            copy.wait()

    @pl.when(jnp.logical_and(c == 0, units > 0))
    def _prime():
        start_weights(0, 0, 0)

    for s in range(rows // FFN_UNIT):
        rs = slice(s * FFN_UNIT, (s + 1) * FFN_UNIT)
        if dense:
            x = x_ref[rs, :]
            live = (s * FFN_UNIT + lax.broadcasted_iota(jnp.int32, (FFN_UNIT, 1), 0)) < nvalid
            o_ref[rs, :] = x
            maybe_xb_scr[rs, :] = jnp.where(live, _rms(x, gain_ref[...]), 0.0).astype(BF16)
        else:
            o_ref[rs, :] = jnp.zeros((FFN_UNIT, o_ref.shape[1]), F32)

    def sub_tile(slot, start, nrows):
        rs = pl.ds(pl.multiple_of(start, FFN_UNIT), nrows)
        xs = maybe_xb_scr[rs, :] if dense else x_ref[rs, :].astype(BF16)
        gate = _bdot(xs, wg_buf[slot].astype(BF16))
        up = _bdot(xs, wu_buf[slot].astype(BF16))
        act = (gate * jax.nn.sigmoid(gate) * up).astype(BF16)
        o_ref[rs, :] += _bdot(act, wd_buf[slot].astype(BF16))

    def swiglu_block(slot):
        full = rows // FFN_UNIT
        straight = tuple(range(full - (1 if dense else FFN_STRAIGHT) + 1, full + 1))
        for n in straight:
            @pl.when(units == n)
            def _straight(n=n):
                sub_tile(slot, 0, n * FFN_UNIT)

        @pl.when(units < straight[0])
        def _partial():
            ngroup = lax.shift_right_logical(units, FFN_GROUP.bit_length() - 1)

            def group(s, carry):
                sub_tile(slot, s * (FFN_GROUP * FFN_UNIT), FFN_GROUP * FFN_UNIT)
                return carry

            lax.fori_loop(0, ngroup, group, 0)
            done = ngroup * FFN_GROUP
            for width in [FFN_GROUP >> k for k in range(1, FFN_GROUP.bit_length())]:
                bit = lax.bitwise_and(units, width)

                @pl.when(bit != 0)
                def _rest(width=width, done=done):
                    sub_tile(slot, done * FFN_UNIT, width * FFN_UNIT)

                done = done + bit

    @pl.when(units > 0)
    def _work():
        nxt = jnp.minimum(c + 1, nch - 1)
        next_live = jnp.logical_and(c + 1 < nch, nu_ref[nxt] > 0)

        def block(f, carry):
            slot = lax.bitwise_and(f, 1)
            wait_weights(c, f, slot)

            @pl.when(f + 1 < nf)
            def _():
                start_weights(c, f + 1, 1 - slot)

            @pl.when(jnp.logical_and(f + 1 == nf, next_live))
            def _():
                start_weights(nxt, 0, 1 - slot)

            swiglu_block(slot)
            return carry

        lax.fori_loop(0, nf, block, 0)


def _ffn(x, gain, wg, wu, wd, chunk_expert, chunk_units, chunk_nvalid, *, dense):
    m, d = x.shape
    dff = wg.shape[-1]
    nch = chunk_expert.shape[0]
    nf = dff // FFN_TF
    assert dff % FFN_TF == 0 and nf % 2 == 0
    last_live = jnp.maximum(jnp.sum(chunk_units > 0) - 1, 0).astype(jnp.int32)
    x_block = jnp.minimum(jnp.arange(nch, dtype=jnp.int32), last_live)
    vm = (2 * FFN_ROWS * d * 4 + FFN_ROWS * d * 2 + 3 * 2 * d * FFN_TF * 4
          + 3 * d * FFN_TF * 2 + 16 * FFN_UNIT * FFN_TF * 4)
    once = pl.Buffered(1)
    grid_spec = pltpu.PrefetchScalarGridSpec(
        num_scalar_prefetch=4,
        grid=(nch,),
        in_specs=[
            pl.BlockSpec((FFN_ROWS, d), lambda c, ce, nu, nv, xb: (xb[c], 0), pipeline_mode=once),
            pl.BlockSpec((1, d), lambda c, ce, nu, nv, xb: (0, 0)),
            pl.BlockSpec(memory_space=pl.ANY),
            pl.BlockSpec(memory_space=pl.ANY),
            pl.BlockSpec(memory_space=pl.ANY),
        ],
        out_specs=pl.BlockSpec((FFN_ROWS, d), lambda c, ce, nu, nv, xb: (c, 0),
                               pipeline_mode=once if dense else None),
        scratch_shapes=[pltpu.VMEM((2, d, FFN_TF), F32), pltpu.VMEM((2, d, FFN_TF), F32),
                        pltpu.VMEM((2, FFN_TF, d), F32), pltpu.SemaphoreType.DMA((3, 2))]
        + ([pltpu.VMEM((FFN_ROWS, d), BF16)] if dense else []),
    )
    return pl.pallas_call(
        functools.partial(_ffn_kernel, dense=dense, nf=nf),
        grid_spec=grid_spec,
        out_shape=jax.ShapeDtypeStruct((m, d), F32),
        compiler_params=pltpu.CompilerParams(
            dimension_semantics=("arbitrary",),
            vmem_limit_bytes=_vmem_limit(vm)),
        name="ffn_dense" if dense else "ffn_experts",
    )(chunk_expert, chunk_units, chunk_nvalid, x_block, x, gain.reshape(1, d), wg, wu, wd)


def _dense_ffn(x, gain, wg, wu, wd, layer):
    m = x.shape[0]
    nch = pl.cdiv(m, FFN_ROWS)
    nvalid = jnp.minimum(FFN_ROWS, m - FFN_ROWS * jnp.arange(nch, dtype=jnp.int32))
    units = (nvalid + FFN_UNIT - 1) // FFN_UNIT
    which = jnp.full((nch,), layer, jnp.int32)
    return _ffn(x, gain, wg, wu, wd, which, units, nvalid, dense=True)


def _route_kernel(x_ref, g_ref, r_ref, hn_ref, info_ref, fields_ref, cnt_ref, carry_scr, *, n_experts):
    i = pl.program_id(0)
    tm = x_ref.shape[0]

    @pl.when(i == 0)
    def _():
        carry_scr[...] = jnp.zeros_like(carry_scr)

    hn = _rms(x_ref[...], g_ref[...])
    hn_ref[...] = hn
    lane = lax.broadcasted_iota(jnp.int32, (tm, LANES), 1).astype(F32)
    logits = jnp.where(lane < n_experts, _split_dot(hn, r_ref[...]), -jnp.inf)
    m1 = jnp.max(logits, axis=-1, keepdims=True)
    i1 = jnp.min(jnp.where(logits == m1, lane, float(LANES)), axis=-1, keepdims=True)
    rest = jnp.where(lane == i1, -jnp.inf, logits)
    m2 = jnp.max(rest, axis=-1, keepdims=True)
    i2 = jnp.min(jnp.where(rest == m2, lane, float(LANES)), axis=-1, keepdims=True)
    e = jnp.exp(m2 - m1)
    g1 = 1.0 / (1.0 + e)
    g2 = e / (1.0 + e)
    oh1 = (lane == i1).astype(F32)
    oh2 = (lane == i2).astype(F32)
    cnt = oh1 + oh2
    row = lax.broadcasted_iota(jnp.int32, (tm, tm), 0)
    col = lax.broadcasted_iota(jnp.int32, (tm, tm), 1)
    before = (col < row).astype(BF16)
    ranks = _bdot(before, cnt.astype(BF16)) + carry_scr[0:1, :]
    rank1 = jnp.sum(ranks * oh1, axis=-1, keepdims=True)
    rank2 = jnp.sum(ranks * oh2, axis=-1, keepdims=True)
    carry_scr[0:1, :] = carry_scr[0:1, :] + jnp.sum(cnt, axis=0, keepdims=True)
    cnt_ref[...] = jnp.broadcast_to(carry_scr[0:1, :], cnt_ref.shape)
    info = jnp.where(lane == 0, i1, 0.0)
    info = jnp.where(lane == 1, i2, info)
    info = jnp.where(lane == 2, g1, info)
    info = jnp.where(lane == 3, g2, info)
    info = jnp.where(lane == 4, rank1, info)
    info = jnp.where(lane == 5, rank2, info)
    info_ref[...] = info
    fields_ref[...] = info.T[:fields_ref.shape[0]]


def _route(x, gain, router):
    m, d = x.shape
    ne = router.shape[1]
    rpad = jnp.pad(router, ((0, 0), (0, LANES - ne)))
    vm = 4 * TM * d * 4 + 2 * d * LANES * 4 + 4 * TM * LANES * 4 + TM * TM * 8 + 8 * TM * d * 4
    return pl.pallas_call(
        functools.partial(_route_kernel, n_experts=ne),
        grid=(m // TM,),
        in_specs=[
            pl.BlockSpec((TM, d), lambda i: (i, 0)),
            pl.BlockSpec((1, d), lambda i: (0, 0)),
            pl.BlockSpec((d, LANES), lambda i: (0, 0)),
        ],
        out_specs=[pl.BlockSpec((TM, d), lambda i: (i, 0)),
                   pl.BlockSpec((TM, LANES), lambda i: (i, 0)),
                   pl.BlockSpec((SUBLANES, TM), lambda i: (0, i)),
                   pl.BlockSpec((SUBLANES, LANES), lambda i: (0, 0))],
        out_shape=[jax.ShapeDtypeStruct((m, d), F32), jax.ShapeDtypeStruct((m, LANES), F32),
                   jax.ShapeDtypeStruct((SUBLANES, m), F32), jax.ShapeDtypeStruct((SUBLANES, LANES), F32)],
        scratch_shapes=[pltpu.VMEM((SUBLANES, LANES), F32)],
        compiler_params=pltpu.CompilerParams(
            dimension_semantics=("arbitrary",), vmem_limit_bytes=_vmem_limit(vm)),
        name="route",
    )(x, gain.reshape(1, d), rpad)


def _row_copy(src, src_row, dst, dst_row, sem):
    return pltpu.make_async_copy(src.at[pl.ds(src_row, 1), :], dst.at[pl.ds(dst_row, 1), :], sem)


def _dispatch_kernel(p1_ref, p2_ref, hn_ref, init_ref, xs_ref, sem):
    del init_ref
    base = pl.program_id(0) * hn_ref.shape[0]
    tm = hn_ref.shape[0]

    def start(r, carry):
        _row_copy(hn_ref, r, xs_ref, p1_ref[base + r], sem).start()
        _row_copy(hn_ref, r, xs_ref, p2_ref[base + r], sem).start()
        return carry

    def wait(r, carry):
        _row_copy(hn_ref, r, xs_ref, p1_ref[base + r], sem).wait()
        _row_copy(hn_ref, r, xs_ref, p2_ref[base + r], sem).wait()
        return carry

    lax.fori_loop(0, tm, start, 0, unroll=ROW_DMA_UNROLL)
    lax.fori_loop(0, tm, wait, 0, unroll=ROW_DMA_UNROLL)


def _dispatch(hn, pos1, pos2, init):
    m, d = hn.shape
    n_rows = init.shape[0]
    assert init.shape == (n_rows, d) and init.dtype == F32 and m % DISPATCH_TM == 0
    grid_spec = pltpu.PrefetchScalarGridSpec(
        num_scalar_prefetch=2,
        grid=(m // DISPATCH_TM,),
        in_specs=[pl.BlockSpec((DISPATCH_TM, d), lambda i, p1, p2: (i, 0)),
                  pl.BlockSpec(memory_space=pl.ANY)],
        out_specs=pl.BlockSpec(memory_space=pl.ANY),
        scratch_shapes=[pltpu.SemaphoreType.DMA(())],
    )
    return pl.pallas_call(
        _dispatch_kernel,
        grid_spec=grid_spec,
        out_shape=jax.ShapeDtypeStruct((n_rows, d), F32),
        input_output_aliases={3: 0},
        compiler_params=pltpu.CompilerParams(dimension_semantics=("arbitrary",)),
        name="dispatch",
    )(pos1, pos2, hn, init)


def _combine_kernel(p1_ref, p2_ref, x_ref, info_ref, gain_ref, ys_ref, *refs, final_norm, head_tiles):
    *out_refs, b1_scr, b2_scr, sems = refs
    tm = x_ref.shape[0]
    i = pl.program_id(0)
    slot = lax.bitwise_and(i, 1)

    def gather(tile, sl, go):
        def row(r, carry):
            t = tile * tm + r
            for pos_ref, buf in ((p1_ref, b1_scr), (p2_ref, b2_scr)):
                copy = _row_copy(ys_ref, pos_ref[t], buf.at[sl], r, sems.at[sl])
                copy.start() if go else copy.wait()
            return carry

        lax.fori_loop(0, tm, row, 0, unroll=ROW_DMA_UNROLL)

    @pl.when(i == 0)
    def _():
        gather(0, 0, True)

    @pl.when(i + 1 < pl.num_programs(0))
    def _():
        gather(i + 1, 1 - slot, True)

    gather(i, slot, False)
    info = info_ref[...]
    e1, e2 = info[:, 0:1], info[:, 1:2]
    g1, g2 = info[:, 2:3], info[:, 3:4]
    y1 = g1 * b1_scr[slot]
    y2 = g2 * b2_scr[slot]
    lo = jnp.where(e1 < e2, y1, y2)
    hi = jnp.where(e1 < e2, y2, y1)
    out = x_ref[...] + (lo + hi)
    if final_norm:
        out = _rms(out, gain_ref[...])
    if head_tiles is None:
        out_refs[0][...] = out
    else:
        head_ref, tail_ref = out_refs

        @pl.when(i < head_tiles)
        def _():
            head_ref[...] = out

        @pl.when(i >= head_tiles)
        def _():
            tail_ref[...] = out


def _combine(x, info, ys, pos1, pos2, gain, *, final_norm, head_rows=None):
    m, d = x.shape
    vm = 6 * ROUTE_TM * d * 4 + 2 * ROUTE_TM * LANES * 4 + 4 * ROUTE_TM * d * 4 + 6 * ROUTE_TM * d * 4
    if head_rows is None:
        head_tiles = None
        out_specs = pl.BlockSpec((ROUTE_TM, d), lambda i, p1, p2: (i, 0))
        out_shape = jax.ShapeDtypeStruct((m, d), F32)
    else:
        assert head_rows % ROUTE_TM == 0 and 0 < head_rows < m
        head_tiles = head_rows // ROUTE_TM
        out_specs = [pl.BlockSpec((ROUTE_TM, d), lambda i, p1, p2: (jnp.minimum(i, head_tiles - 1), 0)),
                     pl.BlockSpec((ROUTE_TM, d), lambda i, p1, p2: (jnp.maximum(i - head_tiles, 0), 0))]
        out_shape = [jax.ShapeDtypeStruct((head_rows, d), F32), jax.ShapeDtypeStruct((m - head_rows, d), F32)]
    grid_spec = pltpu.PrefetchScalarGridSpec(
        num_scalar_prefetch=2,
        grid=(m // ROUTE_TM,),
        in_specs=[
            pl.BlockSpec((ROUTE_TM, d), lambda i, p1, p2: (i, 0)),
            pl.BlockSpec((ROUTE_TM, LANES), lambda i, p1, p2: (i, 0)),
            pl.BlockSpec((1, d), lambda i, p1, p2: (0, 0)),
            pl.BlockSpec(memory_space=pl.ANY),
        ],
        out_specs=out_specs,
        scratch_shapes=[pltpu.VMEM((2, ROUTE_TM, d), F32), pltpu.VMEM((2, ROUTE_TM, d), F32),
                        pltpu.SemaphoreType.DMA((2,))],
    )
    return pl.pallas_call(
        functools.partial(_combine_kernel, final_norm=final_norm, head_tiles=head_tiles),
        grid_spec=grid_spec,
        out_shape=out_shape,
        compiler_params=pltpu.CompilerParams(
            dimension_semantics=("arbitrary",), vmem_limit_bytes=_vmem_limit(vm)),
        name="combine_final" if final_norm else "combine",
    )(pos1, pos2, x, info, gain.reshape(1, d), ys)


def _moe_ffn(x, gain, router, wg, wu, wd, layer, out_gain, *, final_norm, head_rows=None, spare=None):
    m, d = x.shape
    ne = router.shape[1]
    wg, wu, wd = (w.reshape((-1,) + w.shape[2:]) for w in (wg, wu, wd))
    assert m % TM == 0 and m % ROUTE_TM == 0
    hn, info, fields, cnt = _route(x, gain, router)
    fields = fields.astype(jnp.int32)
    counts = cnt[0, :ne].astype(jnp.int32)
    chunks_per = (counts + FFN_ROWS - 1) // FFN_ROWS
    chunk_end = jnp.cumsum(chunks_per)
    chunk_start = chunk_end - chunks_per
    share = (counts + jnp.maximum(chunks_per, 1) - 1) // jnp.maximum(chunks_per, 1)
    share = jnp.maximum((share + FFN_UNIT - 1) // FFN_UNIT * FFN_UNIT, FFN_UNIT)

    def position(e, rank):
        j = rank // share[e]
        return (chunk_start[e] + j) * FFN_ROWS + rank - j * share[e]

    pos1 = position(fields[0], fields[4])
    pos2 = position(fields[1], fields[5])
    nch = (m * TOP_K) // FFN_ROWS + ne
    cidx = jnp.arange(nch, dtype=jnp.int32)
    last = jnp.maximum(chunk_end[-1] - 1, 0)
    owner = jnp.sum(jnp.minimum(cidx, last)[:, None] >= chunk_end[None, :], axis=1).astype(jnp.int32)
    owner = jnp.minimum(owner, ne - 1)
    local = cidx - chunk_start[owner]
    nvalid = jnp.clip(counts[owner] - local * share[owner], 0, share[owner])
    nvalid = jnp.where(cidx < chunk_end[-1], nvalid, 0).astype(jnp.int32)
    units = (nvalid + FFN_UNIT - 1) // FFN_UNIT
    if spare is None:
        spare = jnp.zeros((nch * FFN_ROWS, d), F32)
    xs = _dispatch(hn, pos1, pos2, spare)
    ys = _ffn(xs, gain, wg, wu, wd, owner + layer * ne, units, nvalid, dense=False)
    out = _combine(x, info, ys, pos1, pos2, out_gain, final_norm=final_norm, head_rows=head_rows)
    return out, ys


def kernel(x_prompt, x_sample, state_conv, state_pool, norm_mix, norm_ffn, final_norm, a_w_in, a_v_gain, a_w_s, a_b_s, a_w_out, b_w_in, b_conv, b_w_out, c_w_group, c_scale, ffn_w_gate, ffn_w_up, ffn_w_down, moe_router, moe_w_gate, moe_w_up, moe_w_down):
    batch, seq, d = x_prompt.shape
    slab, steps, _ = x_sample.shape
    depth = norm_mix.shape[0]
    assert depth % 2 == 0, "the final RMSNorm is fused into the routed combine kernel of the last layer"
    m_p = batch * seq
    geo = dict(m_p=m_p, slab=slab, steps=steps)

    def to_time_major(a):
        return a.transpose(1, 0, 2).reshape(a.shape[1] * slab, a.shape[2])

    def from_time_major(a, r):
        return a.reshape(r, slab, a.shape[-1]).transpose(1, 0, 2)

    def seq_tails(a, r):
        return jnp.stack([a[(b + 1) * seq - r:(b + 1) * seq] for b in range(batch)])

    a_w_out_b = a_w_out.astype(BF16)
    b_w_out_b = b_w_out.astype(BF16)
    x = jnp.concatenate([x_prompt.reshape(m_p, d), to_time_major(x_sample)], axis=0)
    chunk_v_s, conv_p, conv_s, pool_p, pool_s = [], [], [], [], []
    spare = None
    for i in range(depth):
        j, kind = divmod(i, 3)
        if kind == 0:
            z = _inproj_gelu(x, norm_mix[i], a_w_in, j)
            x, cv = _chunk_out(z, x, a_v_gain[j], a_w_s[j], a_b_s[j], a_w_out_b, j, **geo)
            chunk_v_s.append(from_time_major(cv, steps))
        elif kind == 1:
            hist = b_conv.shape[1] - 1
            bg, zc = _inproj_conv(x, norm_mix[i], b_w_in, j)
            x = _conv_out(bg, zc, x, to_time_major(state_conv[j]), b_conv[j], b_w_out_b, j, seq=seq, **geo)
            conv_p.append(seq_tails(zc, hist))
            conv_s.append(from_time_major(zc[m_p:], steps)[:, steps - hist:])
        else:
            nbuf = state_pool.shape[2]
            x, tails, hn_s = _pool_mixer(x, norm_mix[i], to_time_major(state_pool[j]), c_w_group[j],
                                         c_scale[j], seq=seq, **geo)
            per_tile = tails.shape[0] // (m_p // TM)
            ends = tails.reshape(batch, seq // TM, per_tile, d)[:, -1]
            pool_p.append(ends[:, per_tile - nbuf:])
            hc = jnp.concatenate([state_pool[j], from_time_major(hn_s, steps)], axis=1)
            pool_s.append(hc[:, hc.shape[1] - nbuf:])
        f = i // 2
        if i % 2 == 0:
            x = _dense_ffn(x, norm_ffn[i], ffn_w_gate, ffn_w_up, ffn_w_down, f)
        else:
            last = i == depth - 1
            x, spare = _moe_ffn(x, norm_ffn[i], moe_router[f], moe_w_gate, moe_w_up, moe_w_down, f,
                                final_norm, final_norm=last, head_rows=m_p if last else None, spare=spare)
    y_prompt = x[0].reshape(batch, seq, d)
    y_sample = from_time_major(x[1], steps)
    return (y_prompt, y_sample, jnp.stack(chunk_v_s), jnp.stack(conv_p), jnp.stack(conv_s),
            jnp.stack(pool_p), jnp.stack(pool_s))
```

```python
import functools
import math

import jax
import jax.numpy as jnp
from jax import lax
from jax.experimental import pallas as pl
from jax.experimental.pallas import tpu as pltpu

F32 = jnp.float32
BF16 = jnp.bfloat16
EPS = 1e-6
PAST_LEN = 16384
POOL_WINDOWS = (2, 4, 8, 16)
TOP_K = 2

V7X_VMEM_BYTES = 64 * 1024 * 1024
SUBLANES = 8
LANES = 128

TM = 512
TN = 512
FFN_UNIT = 64
FFN_ROWS = 18 * FFN_UNIT
FFN_STRAIGHT = 3
FFN_GROUP = 8
FFN_TF = 512
DISPATCH_TM = 512
ROUTE_TM = 256
ROW_DMA_UNROLL = 32


def _vmem_limit(nbytes):
    return min(int(nbytes) + (6 << 20), V7X_VMEM_BYTES - (4 << 20))


def _rms(x, gain):
    ms = jnp.mean(x * x, axis=-1, keepdims=True)
    return x * lax.rsqrt(ms + EPS) * gain


def _gelu_tanh(x):
    c = math.sqrt(2.0 / math.pi)
    return 0.5 * x * (1.0 + jnp.tanh(c * (x + 0.044715 * (x * x * x))))


def _bdot(a, b):
    return jnp.dot(a, b, preferred_element_type=F32)


def _split_dot(a, b):
    ah = a.astype(BF16)
    al = (a - ah.astype(F32)).astype(BF16)
    bh = b.astype(BF16)
    bl = (b - bh.astype(F32)).astype(BF16)
    return _bdot(ah, bh) + _bdot(ah, bl) + _bdot(al, bh)


INPROJ_ROW_TILES = 8


def _inproj_rows(m):
    assert m % (INPROJ_ROW_TILES * 2 * SUBLANES) == 0
    return m // INPROJ_ROW_TILES


def _inproj_gelu_kernel(x_ref, g_ref, w_ref, o_ref, h_scr):
    @pl.when(pl.program_id(1) == 0)
    def _():
        h_scr[...] = _rms(x_ref[...], g_ref[...]).astype(BF16)

    z = _bdot(h_scr[...], w_ref[0].astype(BF16))
    o_ref[...] = _gelu_tanh(z)


def _inproj_gelu(x, gain, w, layer):
    m, d = x.shape
    n = w.shape[2]
    tm = _inproj_rows(m)
    tn = 2 * TN
    vm = 2 * tm * d * 4 + tm * d * 2 + 2 * d * tn * 4 + d * tn * 2 + 4 * tm * tn * 4
    return pl.pallas_call(
        _inproj_gelu_kernel,
        grid=(m // tm, n // tn),
        in_specs=[
            pl.BlockSpec((tm, d), lambda i, j: (i, 0)),
            pl.BlockSpec((1, d), lambda i, j: (0, 0)),
            pl.BlockSpec((1, d, tn), lambda i, j: (layer, 0, j)),
        ],
        out_specs=pl.BlockSpec((tm, tn), lambda i, j: (i, j)),
        out_shape=jax.ShapeDtypeStruct((m, n), F32),
        scratch_shapes=[pltpu.VMEM((tm, d), BF16)],
        compiler_params=pltpu.CompilerParams(
            dimension_semantics=("arbitrary", "arbitrary"),
            vmem_limit_bytes=_vmem_limit(vm)),
        name="inproj_gelu",
    )(x, gain.reshape(1, d), w)


def _inproj_conv_kernel(x_ref, g_ref, wb_ref, wc_ref, wh_ref, ob_ref, oz_ref, h_scr):
    @pl.when(pl.program_id(1) == 0)
    def _():
        h_scr[...] = _rms(x_ref[...], g_ref[...]).astype(BF16)

    h = h_scr[...]
    ob_ref[...] = _bdot(h, wb_ref[0].astype(BF16))
    c = _bdot(h, wc_ref[0].astype(BF16))
    hin = _bdot(h, wh_ref[0].astype(BF16))
    oz_ref[...] = c * hin


def _inproj_conv(x, gain, w, layer):
    m, d = x.shape
    cd = w.shape[2] // 3
    tm = _inproj_rows(m)
    tn = TN // 2
    nb = cd // tn
    vm = 2 * tm * d * 4 + tm * d * 2 + 3 * (2 * d * tn * 4 + d * tn * 2) + 10 * tm * tn * 4
    out = jax.ShapeDtypeStruct((m, cd), F32)
    return pl.pallas_call(
        _inproj_conv_kernel,
        grid=(m // tm, nb),
        in_specs=[
            pl.BlockSpec((tm, d), lambda i, j: (i, 0)),
            pl.BlockSpec((1, d), lambda i, j: (0, 0)),
            pl.BlockSpec((1, d, tn), lambda i, j: (layer, 0, j)),
            pl.BlockSpec((1, d, tn), lambda i, j: (layer, 0, j + nb)),
            pl.BlockSpec((1, d, tn), lambda i, j: (layer, 0, j + 2 * nb)),
        ],
        out_specs=[pl.BlockSpec((tm, tn), lambda i, j: (i, j)),
                   pl.BlockSpec((tm, tn), lambda i, j: (i, j))],
        out_shape=[out, out],
        scratch_shapes=[pltpu.VMEM((tm, d), BF16)],
        compiler_params=pltpu.CompilerParams(
            dimension_semantics=("arbitrary", "arbitrary"),
            vmem_limit_bytes=_vmem_limit(vm)),
        name="inproj_conv",
    )(x, gain.reshape(1, d), w, w, w)


def _chunk_out_kernel(u_ref, v_ref, vg_ref, gw_ref, gb_ref, sw_ref, sb_ref, w_ref, x_ref,
                      o_ref, cv_ref, p_scr, *, np_tiles, chunk, groups, slab, steps):
    i = pl.program_id(0)
    gd = p_scr.shape[1] // groups

    @pl.when(i < np_tiles)
    def _prompt():
        row = lax.broadcasted_iota(jnp.int32, (chunk, chunk), 0)
        col = lax.broadcasted_iota(jnp.int32, (chunk, chunk), 1)
        causal = col <= row
        for s in range(p_scr.shape[0] // chunk):
            rs = slice(s * chunk, (s + 1) * chunk)
            vb = _rms(v_ref[rs, :], vg_ref[...]).astype(BF16)
            for h in range(groups):
                cs = slice(h * gd, (h + 1) * gd)
                wm = jnp.where(causal, gw_ref[h], 0.0).astype(BF16)
                mixed = _bdot(wm, vb[:, cs]) + gb_ref[:, h:h + 1]
                p_scr[rs, cs] = (u_ref[rs, cs] * mixed).astype(BF16)

    @pl.when(i >= np_tiles)
    def _sample():
        for t in range(steps):
            rs = slice(t * slab, (t + 1) * slab)
            cv_ref[rs, :] = _rms(v_ref[rs, :], vg_ref[...])
        for t in range(steps):
            rs = slice(t * slab, (t + 1) * slab)
            for h in range(groups):
                cs = slice(h * gd, (h + 1) * gd)
                mixed = sw_ref[t * steps:t * steps + 1, cs] * cv_ref[0:slab, cs]
                for s in range(1, t + 1):
                    k = t * steps + s
                    mixed = mixed + sw_ref[k:k + 1, cs] * cv_ref[s * slab:(s + 1) * slab, cs]
                mixed = mixed + sb_ref[t:t + 1, cs]
                p_scr[rs, cs] = (u_ref[rs, cs] * mixed).astype(BF16)

    o_ref[...] = x_ref[...] + _bdot(p_scr[...], w_ref[0])


def _chunk_out(z, x, v_gain, w_s, b_s, w_out, layer, *, m_p, slab, steps):
    m, d = x.shape
    a = z.shape[1] // 2
    groups, chunk = w_s.shape[0], w_s.shape[1]
    gd = a // groups
    assert m_p % TM == 0 and TM % chunk == 0 and m - m_p == TM == slab * steps
    assert steps <= chunk and slab % SUBLANES == 0 and w_out.dtype == BF16
    np_tiles = m_p // TM
    gb = b_s.T
    sw = jnp.repeat(w_s[:, :steps, :steps].transpose(1, 2, 0).reshape(steps * steps, groups), gd, axis=1)
    sb = jnp.repeat(b_s[:, :steps].T, gd, axis=1)
    vm = (4 * TM * a * 4 + TM * a * 2 + 2 * groups * chunk * chunk * 4 + a * d * 2
          + 4 * TM * d * 4 + TM * a * 4 + 4 * (steps * steps + steps) * a * 4 + 2 * TM * d * 4)
    once = pl.Buffered(1)
    kern = functools.partial(_chunk_out_kernel, np_tiles=np_tiles, chunk=chunk, groups=groups,
                             slab=slab, steps=steps)
    return pl.pallas_call(
        kern,
        grid=(m // TM,),
        in_specs=[
            pl.BlockSpec((TM, a), lambda i: (i, 0)),
            pl.BlockSpec((TM, a), lambda i: (i, 1)),
            pl.BlockSpec((1, a), lambda i: (0, 0)),
            pl.BlockSpec((groups, chunk, chunk), lambda i: (0, 0, 0)),
            pl.BlockSpec((chunk, groups), lambda i: (0, 0)),
            pl.BlockSpec((steps * steps, a), lambda i: (0, 0)),
            pl.BlockSpec((steps, a), lambda i: (0, 0)),
            pl.BlockSpec((1, a, d), lambda i: (layer, 0, 0), pipeline_mode=once),
            pl.BlockSpec((TM, d), lambda i: (i, 0)),
        ],
        out_specs=[pl.BlockSpec((TM, d), lambda i: (i, 0)),
                   pl.BlockSpec((TM, a), lambda i: (0, 0), pipeline_mode=once)],
        out_shape=[jax.ShapeDtypeStruct((m, d), F32), jax.ShapeDtypeStruct((TM, a), F32)],
        scratch_shapes=[pltpu.VMEM((TM, a), BF16)],
        compiler_params=pltpu.CompilerParams(
            dimension_semantics=("arbitrary",),
            vmem_limit_bytes=_vmem_limit(vm)),
        name="chunk_out",
    )(z, z, v_gain.reshape(1, a), w_s, gb, sw, sb, w_out, x)


def _conv_out_kernel(bg_ref, z_ref, halo_ref, st_ref, cw_ref, w_ref, x_ref, o_ref, p_scr,
                     *, np_tiles, tiles_per_seq, slab, steps):
    i = pl.program_id(0)
    width = cw_ref.shape[0]
    nhalo = halo_ref.shape[0]
    tm, cdim = p_scr.shape

    @pl.when(i < np_tiles)
    def _prompt():
        keep = (i % tiles_per_seq != 0).astype(F32)
        for c in range(cdim // TN):
            cs = slice(c * TN, (c + 1) * TN)
            ext = jnp.concatenate([halo_ref[:, cs] * keep, z_ref[:, cs]], axis=0)
            conv = cw_ref[0:1, cs] * pltpu.roll(ext, width - 1, axis=0)[nhalo:]
            for k in range(1, width):
                sh = width - 1 - k
                zk = pltpu.roll(ext, sh, axis=0)[nhalo:] if sh else z_ref[:, cs]
                conv = conv + cw_ref[k:k + 1, cs] * zk
            p_scr[:, cs] = (bg_ref[:, cs] * conv).astype(BF16)

    @pl.when(i >= np_tiles)
    def _sample():
        hist = width - 1

        def zrow(r, cs):
            if r < hist:
                return st_ref[r * slab:(r + 1) * slab, cs]
            return z_ref[(r - hist) * slab:(r - hist + 1) * slab, cs]

        for c in range(cdim // TN):
            cs = slice(c * TN, (c + 1) * TN)
            for t in range(steps):
                conv = cw_ref[0:1, cs] * zrow(t, cs)
                for k in range(1, width):
                    conv = conv + cw_ref[k:k + 1, cs] * zrow(t + k, cs)
                rs = slice(t * slab, (t + 1) * slab)
                p_scr[rs, cs] = (bg_ref[rs, cs] * conv).astype(BF16)

    o_ref[...] = x_ref[...] + _bdot(p_scr[...], w_ref[0])


def _conv_out(bg, zc, x, state_t, conv_w, w_out, layer, *, m_p, seq, slab, steps):
    m, d = x.shape
    cd = zc.shape[1]
    width = conv_w.shape[0]
    assert m - m_p == TM == slab * steps and seq % TM == 0 and width - 1 <= SUBLANES
    assert w_out.dtype == BF16
    np_tiles = m_p // TM
    halo_blocks = TM // SUBLANES
    vm = (4 * TM * cd * 4 + 2 * SUBLANES * cd * 4 + (width - 1) * slab * cd * 4 + TM * cd * 2
          + cd * d * 2 + 4 * TM * d * 4 + 2 * TM * d * 4 + 6 * TM * TN * 4)
    kern = functools.partial(_conv_out_kernel, np_tiles=np_tiles, tiles_per_seq=seq // TM,
                             slab=slab, steps=steps)
    once = pl.Buffered(1)
    return pl.pallas_call(
        kern,
        grid=(m // TM,),
        in_specs=[
            pl.BlockSpec((TM, cd), lambda i: (i, 0)),
            pl.BlockSpec((TM, cd), lambda i: (i, 0)),
            pl.BlockSpec((SUBLANES, cd), lambda i: (jnp.maximum(i * halo_blocks - 1, 0), 0)),
            pl.BlockSpec(((width - 1) * slab, cd), lambda i: (0, 0), pipeline_mode=once),
            pl.BlockSpec((width, cd), lambda i: (0, 0)),
            pl.BlockSpec((1, cd, d), lambda i: (layer, 0, 0), pipeline_mode=once),
            pl.BlockSpec((TM, d), lambda i: (i, 0)),
        ],
        out_specs=pl.BlockSpec((TM, d), lambda i: (i, 0)),
        out_shape=jax.ShapeDtypeStruct((m, d), F32),
        scratch_shapes=[pltpu.VMEM((TM, cd), BF16)],
        compiler_params=pltpu.CompilerParams(
            dimension_semantics=("arbitrary",),
            vmem_limit_bytes=_vmem_limit(vm)),
        name="conv_out",
    )(bg, zc, zc, state_t, conv_w, w_out, x)


def _pool_kernel(x_ref, halo_ref, buf_ref, gain_ref, w_ref, sc_ref, o_ref, tail_ref, hs_ref, r_scr,
                 *, np_tiles, tiles_per_seq, slab, steps, windows, pos0):
    i = pl.program_id(0)
    tm = x_ref.shape[0]
    nhalo = halo_ref.shape[0]
    nbuf = buf_ref.shape[0] // slab
    gdim = buf_ref.shape[1]

    @pl.when(i <= np_tiles)
    def _():
        x = x_ref[...]
        r_scr[nhalo:, :] = lax.rsqrt(jnp.mean(x * x, axis=-1, keepdims=True) + EPS)
        hx = halo_ref[...]
        r_scr[:nhalo, :] = lax.rsqrt(jnp.mean(hx * hx, axis=-1, keepdims=True) + EPS)

    def normed(cs):
        return x_ref[:, cs] * r_scr[nhalo:, :] * gain_ref[:, cs]

    def finish(gi, cs, h, pooled):
        diff = (pooled - h).astype(BF16)
        y = _bdot(diff, w_ref[gi].astype(BF16)) * sc_ref[:, cs]
        o_ref[:, cs] = x_ref[:, cs] + y

    @pl.when(i < np_tiles)
    def _prompt():
        seq_tile = i % tiles_per_seq
        keep = (seq_tile != 0).astype(F32)
        pos = seq_tile * tm + lax.broadcasted_iota(jnp.int32, (tm, 1), 0)
        for gi, win in enumerate(windows):
            cs = slice(gi * gdim, (gi + 1) * gdim)
            h = normed(cs)
            tail_ref[:, cs] = h[tm - nhalo:]
            hh = halo_ref[:, cs] * r_scr[:nhalo, :] * gain_ref[:, cs] * keep
            s = jnp.concatenate([hh, h], axis=0)
            k = 1
            while k < win:
                s = s + pltpu.roll(s, k, axis=0)
                k *= 2
            count = jnp.minimum(win, pos + 1).astype(F32)
            finish(gi, cs, h, s[nhalo:] / count)

    for gi, win in enumerate(windows):
        @pl.when(i == np_tiles + gi)
        def _sample(gi=gi, win=win):
            cs = slice(gi * gdim, (gi + 1) * gdim)
            h = normed(cs)
            hs_ref[:, cs] = h

            def hrow(r):
                if r < nbuf:
                    return buf_ref[r * slab:(r + 1) * slab, :]
                return h[(r - nbuf) * slab:(r - nbuf + 1) * slab]

            rows = []
            for t in range(steps):
                acc = hrow(nbuf + t - win + 1)
                for r in range(nbuf + t - win + 2, nbuf + t + 1):
                    acc = acc + hrow(r)
                rows.append(acc / float(min(win, pos0 + t + 1)))
            finish(gi, cs, h, jnp.concatenate(rows, axis=0))


def _pool_mixer(x, gain, buf_t, w_group, scale, *, m_p, seq, slab, steps):
    m, d = x.shape
    ng, gdim = w_group.shape[0], w_group.shape[1]
    nhalo = 2 * SUBLANES
    assert ng == len(POOL_WINDOWS) and max(POOL_WINDOWS) <= nhalo and gdim * ng == d
    assert m - m_p == TM == slab * steps and seq % TM == 0
    assert buf_t.shape[0] // slab >= max(POOL_WINDOWS) - 1
    np_tiles = m_p // TM
    halo_blocks = TM // nhalo
    nbuf_rows = buf_t.shape[0]
    vm = (4 * TM * d * 4 + 2 * nhalo * d * 4 + 2 * nbuf_rows * gdim * 4 + ng * gdim * gdim * 4
          + gdim * gdim * 2 + TM * d * 4 + 2 * nhalo * d * 4 + (TM + nhalo) * LANES * 4 + 12 * TM * gdim * 4)
    kern = functools.partial(_pool_kernel, np_tiles=np_tiles, tiles_per_seq=seq // TM, slab=slab,
                             steps=steps, windows=POOL_WINDOWS, pos0=PAST_LEN)
    tile = lambda i: jnp.minimum(i, np_tiles)
    once = pl.Buffered(1)
    return pl.pallas_call(
        kern,
        grid=(np_tiles + ng,),
        in_specs=[
            pl.BlockSpec((TM, d), lambda i: (tile(i), 0)),
            pl.BlockSpec((nhalo, d), lambda i: (jnp.maximum(tile(i) * halo_blocks - 1, 0), 0)),
            pl.BlockSpec((nbuf_rows, gdim), lambda i: (0, jnp.maximum(i - np_tiles, 0))),
            pl.BlockSpec((1, d), lambda i: (0, 0)),
            pl.BlockSpec((ng, gdim, gdim), lambda i: (0, 0, 0), pipeline_mode=once),
            pl.BlockSpec((1, d), lambda i: (0, 0)),
        ],
        out_specs=[pl.BlockSpec((TM, d), lambda i: (tile(i), 0)),
                   pl.BlockSpec((nhalo, d), lambda i: (jnp.minimum(i, np_tiles - 1), 0)),
                   pl.BlockSpec((TM, d), lambda i: (0, 0), pipeline_mode=once)],
        out_shape=[jax.ShapeDtypeStruct((m, d), F32), jax.ShapeDtypeStruct((np_tiles * nhalo, d), F32),
                   jax.ShapeDtypeStruct((TM, d), F32)],
        scratch_shapes=[pltpu.VMEM((TM + nhalo, 1), F32)],
        compiler_params=pltpu.CompilerParams(
            dimension_semantics=("arbitrary",),
            vmem_limit_bytes=_vmem_limit(vm)),
        name="pool_mixer",
    )(x, x, buf_t, gain.reshape(1, d), w_group, scale.reshape(1, d))


def _ffn_kernel(ce_ref, nu_ref, nv_ref, xb_ref, x_ref, gain_ref, wg_hbm, wu_hbm, wd_hbm, o_ref,
                wg_buf, wu_buf, wd_buf, sems, maybe_xb_scr=None, *, dense, nf):
    del xb_ref
    c = pl.program_id(0)
    nch = pl.num_programs(0)
    rows = x_ref.shape[0]
    nvalid = nv_ref[c]
    units = nu_ref[c]

    def weight_copies(chunk, f, slot):
        e = ce_ref[chunk]
        cols = pl.ds(pl.multiple_of(f * FFN_TF, FFN_TF), FFN_TF)
        return (pltpu.make_async_copy(wg_hbm.at[e, :, cols], wg_buf.at[slot], sems.at[0, slot]),
                pltpu.make_async_copy(wu_hbm.at[e, :, cols], wu_buf.at[slot], sems.at[1, slot]),
                pltpu.make_async_copy(wd_hbm.at[e, cols, :], wd_buf.at[slot], sems.at[2, slot]))

    def start_weights(chunk, f, slot):
        for copy in weight_copies(chunk, f, slot):
            copy.start()

    def wait_weights(chunk, f, slot):
        for copy in weight_copies(chunk, f, slot):
            copy.wait()

    @pl.when(jnp.logical_and(c == 0, units > 0))
    def _prime():
        start_weights(0, 0, 0)

    for s in range(rows // FFN_UNIT):
        rs = slice(s * FFN_UNIT, (s + 1) * FFN_UNIT)
        if dense:
            x = x_ref[rs, :]
            live = (s * FFN_UNIT + lax.broadcasted_iota(jnp.int32, (FFN_UNIT, 1), 0)) < nvalid
            o_ref[rs, :] = x
            maybe_xb_scr[rs, :] = jnp.where(live, _rms(x, gain_ref[...]), 0.0).astype(BF16)
        else:
            o_ref[rs, :] = jnp.zeros((FFN_UNIT, o_ref.shape[1]), F32)

    def sub_tile(slot, start, nrows):
        rs = pl.ds(pl.multiple_of(start, FFN_UNIT), nrows)
        xs = maybe_xb_scr[rs, :] if dense else x_ref[rs, :].astype(BF16)
        gate = _bdot(xs, wg_buf[slot].astype(BF16))
        up = _bdot(xs, wu_buf[slot].astype(BF16))
        act = (gate * jax.nn.sigmoid(gate) * up).astype(BF16)
        o_ref[rs, :] += _bdot(act, wd_buf[slot].astype(BF16))

    def swiglu_block(slot):
        full = rows // FFN_UNIT
        straight = tuple(range(full - (1 if dense else FFN_STRAIGHT) + 1, full + 1))
        for n in straight:
            @pl.when(units == n)
            def _straight(n=n):
                sub_tile(slot, 0, n * FFN_UNIT)

        @pl.when(units < straight[0])
        def _partial():
            ngroup = lax.shift_right_logical(units, FFN_GROUP.bit_length() - 1)

            def group(s, carry):
                sub_tile(slot, s * (FFN_GROUP * FFN_UNIT), FFN_GROUP * FFN_UNIT)
                return carry

            lax.fori_loop(0, ngroup, group, 0)
            done = ngroup * FFN_GROUP
            for width in [FFN_GROUP >> k for k in range(1, FFN_GROUP.bit_length())]:
                bit = lax.bitwise_and(units, width)

                @pl.when(bit != 0)
                def _rest(width=width, done=done):
                    sub_tile(slot, done * FFN_UNIT, width * FFN_UNIT)

                done = done + bit

    @pl.when(units > 0)
    def _work():
        nxt = jnp.minimum(c + 1, nch - 1)
        next_live = jnp.logical_and(c + 1 < nch, nu_ref[nxt] > 0)

        def block(f, carry):
            slot = lax.bitwise_and(f, 1)
            wait_weights(c, f, slot)

            @pl.when(f + 1 < nf)
            def _():
                start_weights(c, f + 1, 1 - slot)

            @pl.when(jnp.logical_and(f + 1 == nf, next_live))
            def _():
                start_weights(nxt, 0, 1 - slot)

            swiglu_block(slot)
            return carry

        lax.fori_loop(0, nf, block, 0)


def _ffn(x, gain, wg, wu, wd, chunk_expert, chunk_units, chunk_nvalid, *, dense):
    m, d = x.shape
    dff = wg.shape[-1]
    nch = chunk_expert.shape[0]
    nf = dff // FFN_TF
    assert dff % FFN_TF == 0 and nf % 2 == 0
    last_live = jnp.maximum(jnp.sum(chunk_units > 0) - 1, 0).astype(jnp.int32)
    x_block = jnp.minimum(jnp.arange(nch, dtype=jnp.int32), last_live)
    vm = (2 * FFN_ROWS * d * 4 + FFN_ROWS * d * 2 + 3 * 2 * d * FFN_TF * 4
          + 3 * d * FFN_TF * 2 + 16 * FFN_UNIT * FFN_TF * 4)
    once = pl.Buffered(1)
    grid_spec = pltpu.PrefetchScalarGridSpec(
        num_scalar_prefetch=4,
        grid=(nch,),
        in_specs=[
            pl.BlockSpec((FFN_ROWS, d), lambda c, ce, nu, nv, xb: (xb[c], 0), pipeline_mode=once),
            pl.BlockSpec((1, d), lambda c, ce, nu, nv, xb: (0, 0)),
            pl.BlockSpec(memory_space=pl.ANY),
            pl.BlockSpec(memory_space=pl.ANY),
            pl.BlockSpec(memory_space=pl.ANY),
        ],
        out_specs=pl.BlockSpec((FFN_ROWS, d), lambda c, ce, nu, nv, xb: (c, 0),
                               pipeline_mode=once if dense else None),
        scratch_shapes=[pltpu.VMEM((2, d, FFN_TF), F32), pltpu.VMEM((2, d, FFN_TF), F32),
                        pltpu.VMEM((2, FFN_TF, d), F32), pltpu.SemaphoreType.DMA((3, 2))]
        + ([pltpu.VMEM((FFN_ROWS, d), BF16)] if dense else []),
    )
    return pl.pallas_call(
        functools.partial(_ffn_kernel, dense=dense, nf=nf),
        grid_spec=grid_spec,
        out_shape=jax.ShapeDtypeStruct((m, d), F32),
        compiler_params=pltpu.CompilerParams(
            dimension_semantics=("arbitrary",),
            vmem_limit_bytes=_vmem_limit(vm)),
        name="ffn_dense" if dense else "ffn_experts",
    )(chunk_expert, chunk_units, chunk_nvalid, x_block, x, gain.reshape(1, d), wg, wu, wd)


def _dense_ffn(x, gain, wg, wu, wd, layer):
    m = x.shape[0]
    nch = pl.cdiv(m, FFN_ROWS)
    nvalid = jnp.minimum(FFN_ROWS, m - FFN_ROWS * jnp.arange(nch, dtype=jnp.int32))
    units = (nvalid + FFN_UNIT - 1) // FFN_UNIT
    which = jnp.full((nch,), layer, jnp.int32)
    return _ffn(x, gain, wg, wu, wd, which, units, nvalid, dense=True)


def _route_kernel(x_ref, g_ref, r_ref, hn_ref, info_ref, fields_ref, cnt_ref, carry_scr, *, n_experts):
    i = pl.program_id(0)
    tm = x_ref.shape[0]

    @pl.when(i == 0)
    def _():
        carry_scr[...] = jnp.zeros_like(carry_scr)

    hn = _rms(x_ref[...], g_ref[...])
    hn_ref[...] = hn
    lane = lax.broadcasted_iota(jnp.int32, (tm, LANES), 1).astype(F32)
    logits = jnp.where(lane < n_experts, _split_dot(hn, r_ref[...]), -jnp.inf)
    m1 = jnp.max(logits, axis=-1, keepdims=True)
    i1 = jnp.min(jnp.where(logits == m1, lane, float(LANES)), axis=-1, keepdims=True)
    rest = jnp.where(lane == i1, -jnp.inf, logits)
    m2 = jnp.max(rest, axis=-1, keepdims=True)
    i2 = jnp.min(jnp.where(rest == m2, lane, float(LANES)), axis=-1, keepdims=True)
    e = jnp.exp(m2 - m1)
    g1 = 1.0 / (1.0 + e)
    g2 = e / (1.0 + e)
    oh1 = (lane == i1).astype(F32)
    oh2 = (lane == i2).astype(F32)
    cnt = oh1 + oh2
    row = lax.broadcasted_iota(jnp.int32, (tm, tm), 0)
    col = lax.broadcasted_iota(jnp.int32, (tm, tm), 1)
    before = (col < row).astype(BF16)
    ranks = _bdot(before, cnt.astype(BF16)) + carry_scr[0:1, :]
    rank1 = jnp.sum(ranks * oh1, axis=-1, keepdims=True)
    rank2 = jnp.sum(ranks * oh2, axis=-1, keepdims=True)
    carry_scr[0:1, :] = carry_scr[0:1, :] + jnp.sum(cnt, axis=0, keepdims=True)
    cnt_ref[...] = jnp.broadcast_to(carry_scr[0:1, :], cnt_ref.shape)
    info = jnp.where(lane == 0, i1, 0.0)
    info = jnp.where(lane == 1, i2, info)
    info = jnp.where(lane == 2, g1, info)
    info = jnp.where(lane == 3, g2, info)
    info = jnp.where(lane == 4, rank1, info)
    info = jnp.where(lane == 5, rank2, info)
    info_ref[...] = info
    fields_ref[...] = info.T[:fields_ref.shape[0]]


def _route(x, gain, router):
    m, d = x.shape
    ne = router.shape[1]
    rpad = jnp.pad(router, ((0, 0), (0, LANES - ne)))
    vm = 4 * TM * d * 4 + 2 * d * LANES * 4 + 4 * TM * LANES * 4 + TM * TM * 8 + 8 * TM * d * 4
    return pl.pallas_call(
        functools.partial(_route_kernel, n_experts=ne),
        grid=(m // TM,),
        in_specs=[
            pl.BlockSpec((TM, d), lambda i: (i, 0)),
            pl.BlockSpec((1, d), lambda i: (0, 0)),
            pl.BlockSpec((d, LANES), lambda i: (0, 0)),
        ],
        out_specs=[pl.BlockSpec((TM, d), lambda i: (i, 0)),
                   pl.BlockSpec((TM, LANES), lambda i: (i, 0)),
                   pl.BlockSpec((SUBLANES, TM), lambda i: (0, i)),
                   pl.BlockSpec((SUBLANES, LANES), lambda i: (0, 0))],
        out_shape=[jax.ShapeDtypeStruct((m, d), F32), jax.ShapeDtypeStruct((m, LANES), F32),
                   jax.ShapeDtypeStruct((SUBLANES, m), F32), jax.ShapeDtypeStruct((SUBLANES, LANES), F32)],
        scratch_shapes=[pltpu.VMEM((SUBLANES, LANES), F32)],
        compiler_params=pltpu.CompilerParams(
            dimension_semantics=("arbitrary",), vmem_limit_bytes=_vmem_limit(vm)),
        name="route",
    )(x, gain.reshape(1, d), rpad)


def _row_copy(src, src_row, dst, dst_row, sem):
    return pltpu.make_async_copy(src.at[pl.ds(src_row, 1), :], dst.at[pl.ds(dst_row, 1), :], sem)


def _dispatch_kernel(p1_ref, p2_ref, hn_ref, init_ref, xs_ref, sem):
    del init_ref
    base = pl.program_id(0) * hn_ref.shape[0]
    tm = hn_ref.shape[0]

    def start(r, carry):
        _row_copy(hn_ref, r, xs_ref, p1_ref[base + r], sem).start(priority=0)
        _row_copy(hn_ref, r, xs_ref, p2_ref[base + r], sem).start(priority=1)
        return carry

    def wait(r, carry):
        _row_copy(hn_ref, r, xs_ref, p1_ref[base + r], sem).wait()
        _row_copy(hn_ref, r, xs_ref, p2_ref[base + r], sem).wait()
        return carry

    lax.fori_loop(0, tm, start, 0, unroll=ROW_DMA_UNROLL)
    lax.fori_loop(0, tm, wait, 0, unroll=ROW_DMA_UNROLL)


def _dispatch(hn, pos1, pos2, init):
    m, d = hn.shape
    n_rows = init.shape[0]
    assert init.shape == (n_rows, d) and init.dtype == F32 and m % DISPATCH_TM == 0
    grid_spec = pltpu.PrefetchScalarGridSpec(
        num_scalar_prefetch=2,
        grid=(m // DISPATCH_TM,),
        in_specs=[pl.BlockSpec((DISPATCH_TM, d), lambda i, p1, p2: (i, 0)),
                  pl.BlockSpec(memory_space=pl.ANY)],
        out_specs=pl.BlockSpec(memory_space=pl.ANY),
        scratch_shapes=[pltpu.SemaphoreType.DMA(())],
    )
    return pl.pallas_call(
        _dispatch_kernel,
        grid_spec=grid_spec,
        out_shape=jax.ShapeDtypeStruct((n_rows, d), F32),
        input_output_aliases={3: 0},
        compiler_params=pltpu.CompilerParams(dimension_semantics=("arbitrary",)),
        name="dispatch",
    )(pos1, pos2, hn, init)


def _combine_kernel(p1_ref, p2_ref, x_ref, info_ref, gain_ref, ys_ref, *refs, final_norm, head_tiles):
    *out_refs, b1_scr, b2_scr, sems = refs
    tm = x_ref.shape[0]
    i = pl.program_id(0)
    slot = lax.bitwise_and(i, 1)

    def gather(tile, sl, go):
        def row(r, carry):
            t = tile * tm + r
            for thread, (pos_ref, buf) in enumerate(((p1_ref, b1_scr), (p2_ref, b2_scr))):
                copy = _row_copy(ys_ref, pos_ref[t], buf.at[sl], r, sems.at[sl])
                copy.start(priority=thread) if go else copy.wait()
            return carry

        lax.fori_loop(0, tm, row, 0, unroll=ROW_DMA_UNROLL)

    @pl.when(i == 0)
    def _():
        gather(0, 0, True)

    @pl.when(i + 1 < pl.num_programs(0))
    def _():
        gather(i + 1, 1 - slot, True)

    gather(i, slot, False)
    info = info_ref[...]
    e1, e2 = info[:, 0:1], info[:, 1:2]
    g1, g2 = info[:, 2:3], info[:, 3:4]
    y1 = g1 * b1_scr[slot]
    y2 = g2 * b2_scr[slot]
    lo = jnp.where(e1 < e2, y1, y2)
    hi = jnp.where(e1 < e2, y2, y1)
    out = x_ref[...] + (lo + hi)
    if final_norm:
        out = _rms(out, gain_ref[...])
    if head_tiles is None:
        out_refs[0][...] = out
    else:
        head_ref, tail_ref = out_refs

        @pl.when(i < head_tiles)
        def _():
            head_ref[...] = out

        @pl.when(i >= head_tiles)
        def _():
            tail_ref[...] = out


def _combine(x, info, ys, pos1, pos2, gain, *, final_norm, head_rows=None):
    m, d = x.shape
    vm = 6 * ROUTE_TM * d * 4 + 2 * ROUTE_TM * LANES * 4 + 4 * ROUTE_TM * d * 4 + 6 * ROUTE_TM * d * 4
    if head_rows is None:
        head_tiles = None
        out_specs = pl.BlockSpec((ROUTE_TM, d), lambda i, p1, p2: (i, 0))
        out_shape = jax.ShapeDtypeStruct((m, d), F32)
    else:
        assert head_rows % ROUTE_TM == 0 and 0 < head_rows < m
        head_tiles = head_rows // ROUTE_TM
        out_specs = [pl.BlockSpec((ROUTE_TM, d), lambda i, p1, p2: (jnp.minimum(i, head_tiles - 1), 0)),
                     pl.BlockSpec((ROUTE_TM, d), lambda i, p1, p2: (jnp.maximum(i - head_tiles, 0), 0))]
        out_shape = [jax.ShapeDtypeStruct((head_rows, d), F32), jax.ShapeDtypeStruct((m - head_rows, d), F32)]
    grid_spec = pltpu.PrefetchScalarGridSpec(
        num_scalar_prefetch=2,
        grid=(m // ROUTE_TM,),
        in_specs=[
            pl.BlockSpec((ROUTE_TM, d), lambda i, p1, p2: (i, 0)),
            pl.BlockSpec((ROUTE_TM, LANES), lambda i, p1, p2: (i, 0)),
            pl.BlockSpec((1, d), lambda i, p1, p2: (0, 0)),
            pl.BlockSpec(memory_space=pl.ANY),
        ],
        out_specs=out_specs,
        scratch_shapes=[pltpu.VMEM((2, ROUTE_TM, d), F32), pltpu.VMEM((2, ROUTE_TM, d), F32),
                        pltpu.SemaphoreType.DMA((2,))],
    )
    return pl.pallas_call(
        functools.partial(_combine_kernel, final_norm=final_norm, head_tiles=head_tiles),
        grid_spec=grid_spec,
        out_shape=out_shape,
        compiler_params=pltpu.CompilerParams(
            dimension_semantics=("arbitrary",), vmem_limit_bytes=_vmem_limit(vm)),
        name="combine_final" if final_norm else "combine",
    )(pos1, pos2, x, info, gain.reshape(1, d), ys)


def _moe_ffn(x, gain, router, wg, wu, wd, layer, out_gain, *, final_norm, head_rows=None, spare=None):
    m, d = x.shape
    ne = router.shape[1]
    wg, wu, wd = (w.reshape((-1,) + w.shape[2:]) for w in (wg, wu, wd))
    assert m % TM == 0 and m % ROUTE_TM == 0
    hn, info, fields, cnt = _route(x, gain, router)
    fields = fields.astype(jnp.int32)
    counts = cnt[0, :ne].astype(jnp.int32)
    chunks_per = (counts + FFN_ROWS - 1) // FFN_ROWS
    chunk_end = jnp.cumsum(chunks_per)
    chunk_start = chunk_end - chunks_per
    share = (counts + jnp.maximum(chunks_per, 1) - 1) // jnp.maximum(chunks_per, 1)
    share = jnp.maximum((share + FFN_UNIT - 1) // FFN_UNIT * FFN_UNIT, FFN_UNIT)

    def position(e, rank):
        j = rank // share[e]
        return (chunk_start[e] + j) * FFN_ROWS + rank - j * share[e]

    pos1 = position(fields[0], fields[4])
    pos2 = position(fields[1], fields[5])
    nch = (m * TOP_K) // FFN_ROWS + ne
    cidx = jnp.arange(nch, dtype=jnp.int32)
    last = jnp.maximum(chunk_end[-1] - 1, 0)
    owner = jnp.sum(jnp.minimum(cidx, last)[:, None] >= chunk_end[None, :], axis=1).astype(jnp.int32)
    owner = jnp.minimum(owner, ne - 1)
    local = cidx - chunk_start[owner]
    nvalid = jnp.clip(counts[owner] - local * share[owner], 0, share[owner])
    nvalid = jnp.where(cidx < chunk_end[-1], nvalid, 0).astype(jnp.int32)
    units = (nvalid + FFN_UNIT - 1) // FFN_UNIT
    if spare is None:
        spare = jnp.zeros((nch * FFN_ROWS, d), F32)
    xs = _dispatch(hn, pos1, pos2, spare)
    ys = _ffn(xs, gain, wg, wu, wd, owner + layer * ne, units, nvalid, dense=False)
    out = _combine(x, info, ys, pos1, pos2, out_gain, final_norm=final_norm, head_rows=head_rows)
    return out, ys


def kernel(x_prompt, x_sample, state_conv, state_pool, norm_mix, norm_ffn, final_norm, a_w_in, a_v_gain, a_w_s, a_b_s, a_w_out, b_w_in, b_conv, b_w_out, c_w_group, c_scale, ffn_w_gate, ffn_w_up, ffn_w_down, moe_router, moe_w_gate, moe_w_up, moe_w_down):
    batch, seq, d = x_prompt.shape
    slab, steps, _ = x_sample.shape
    depth = norm_mix.shape[0]
    assert depth % 2 == 0, "the final RMSNorm is fused into the routed combine kernel of the last layer"
    m_p = batch * seq
    geo = dict(m_p=m_p, slab=slab, steps=steps)

    def to_time_major(a):
        return a.transpose(1, 0, 2).reshape(a.shape[1] * slab, a.shape[2])

    def from_time_major(a, r):
        return a.reshape(r, slab, a.shape[-1]).transpose(1, 0, 2)

    def seq_tails(a, r):
        return jnp.stack([a[(b + 1) * seq - r:(b + 1) * seq] for b in range(batch)])

    a_w_out_b = a_w_out.astype(BF16)
    b_w_out_b = b_w_out.astype(BF16)
    x = jnp.concatenate([x_prompt.reshape(m_p, d), to_time_major(x_sample)], axis=0)
    chunk_v_s, conv_p, conv_s, pool_p, pool_s = [], [], [], [], []
    spare = None
    for i in range(depth):
        j, kind = divmod(i, 3)
        if kind == 0:
            z = _inproj_gelu(x, norm_mix[i], a_w_in, j)
            x, cv = _chunk_out(z, x, a_v_gain[j], a_w_s[j], a_b_s[j], a_w_out_b, j, **geo)
            chunk_v_s.append(from_time_major(cv, steps))
        elif kind == 1:
            hist = b_conv.shape[1] - 1
            bg, zc = _inproj_conv(x, norm_mix[i], b_w_in, j)
            x = _conv_out(bg, zc, x, to_time_major(state_conv[j]), b_conv[j], b_w_out_b, j, seq=seq, **geo)
            conv_p.append(seq_tails(zc, hist))
            conv_s.append(from_time_major(zc[m_p:], steps)[:, steps - hist:])
        else:
            nbuf = state_pool.shape[2]
            x, tails, hn_s = _pool_mixer(x, norm_mix[i], to_time_major(state_pool[j]), c_w_group[j],
                                         c_scale[j], seq=seq, **geo)
            per_tile = tails.shape[0] // (m_p // TM)
            ends = tails.reshape(batch, seq // TM, per_tile, d)[:, -1]
            pool_p.append(ends[:, per_tile - nbuf:])
            hc = jnp.concatenate([state_pool[j], from_time_major(hn_s, steps)], axis=1)
            pool_s.append(hc[:, hc.shape[1] - nbuf:])
        f = i // 2
        if i % 2 == 0:
            x = _dense_ffn(x, norm_ffn[i], ffn_w_gate, ffn_w_up, ffn_w_down, f)
        else:
            last = i == depth - 1
            x, spare = _moe_ffn(x, norm_ffn[i], moe_router[f], moe_w_gate, moe_w_up, moe_w_down, f,
                                final_norm, final_norm=last, head_rows=m_p if last else None, spare=spare)
    y_prompt = x[0].reshape(batch, seq, d)
    y_sample = from_time_major(x[1], steps)
    return (y_prompt, y_sample, jnp.stack(chunk_v_s), jnp.stack(conv_p), jnp.stack(conv_s),
            jnp.stack(pool_p), jnp.stack(pool_s))
```
